```python
import math
import jax, jax.numpy as jnp
from jax import lax
import numpy as np

D_MODEL = 2048
BATCH = 2
SEQ = 16384
DEPTH = 2
DEC_BATCH = 16
DEC_SEQ = 2048
PAST_LEN = 128

N_EVEN = (DEPTH + 1) // 2
N_ODD = DEPTH // 2

ATTN_WIDTH = D_MODEL // 2
CONV_CH = D_MODEL - ATTN_WIDTH
N_DIFF_HEADS = 8
QK_DIM = 64
V_DIM = 2 * QK_DIM
ROPE_DIM = QK_DIM // 4
ROPE_THETA = 500000.0
Q_BLOCK = 128
CONV_K = 3
IN_COLS = 3 * ATTN_WIDTH + 3 * CONV_CH
S5_P = 16
S5_GROUPS = D_MODEL // S5_P
S5_N = 64
S5_GROUPS_PER_BLOCK = 8
FF_DENSE = 5632
N_EXPERTS = 8
TOP_K = 2
FF_EXPERT = 7168
RMS_EPS = 1e-5

kernel_name = "hybrid_diffattn_shortconv_s5_moe_encoder"


def rms_norm(x, g):
    xf = x.astype(jnp.float32)
    y = xf * lax.rsqrt(jnp.mean(xf * xf, axis=-1, keepdims=True) + RMS_EPS)
    return (y * g.astype(jnp.float32)).astype(x.dtype)


def rope_tables(seq):
    inv = 1.0 / (ROPE_THETA ** (jnp.arange(0, ROPE_DIM, 2, dtype=jnp.float32) / ROPE_DIM))
    ang = jnp.arange(seq, dtype=jnp.float32)[:, None] * inv[None, :]
    return jnp.cos(ang), jnp.sin(ang)


def apply_partial_rope(t, cos, sin):
    rot, keep = t[..., :ROPE_DIM], t[..., ROPE_DIM:]
    r1, r2 = rot[..., :ROPE_DIM // 2], rot[..., ROPE_DIM // 2:]
    c = cos[:, None, None, :].astype(t.dtype)
    s = sin[:, None, None, :].astype(t.dtype)
    return jnp.concatenate([r1 * c - r2 * s, r2 * c + r1 * s, keep], axis=-1)


def diff_attention(q, k, v, lam, subln_g, lambda_init):
    B, S = q.shape[0], q.shape[1]
    nblk = S // Q_BLOCK
    scale = QK_DIM ** -0.5
    qb = q.reshape(B, nblk, Q_BLOCK, N_DIFF_HEADS, 2, QK_DIM).transpose(1, 0, 2, 3, 4, 5)
    lam_c = lam.astype(v.dtype)

    def block(qblk):
        s = jnp.einsum('bqhcd,bkhcd->bhcqk', qblk, k, preferred_element_type=jnp.float32) * scale
        p = jax.nn.softmax(s, axis=-1).astype(v.dtype)
        o = jnp.einsum('bhcqk,bkhe->bqhce', p, v)
        return o[..., 0, :] - lam_c * o[..., 1, :]

    o = lax.map(block, qb)
    o = o.transpose(1, 0, 2, 3, 4).reshape(B, S, N_DIFF_HEADS, V_DIM)
    o = rms_norm(o, subln_g) * (1.0 - lambda_init)
    return o.reshape(B, S, N_DIFF_HEADS * V_DIM)


def short_gated_conv(bg, cg, xc, w):
    u = cg * xc
    up = jnp.pad(u, ((0, 0), (1, 1), (0, 0)))
    y = up[:, :-2] * w[0] + up[:, 1:-1] * w[1] + up[:, 2:] * w[2]
    return bg * y


def _ssm_combine(e_i, e_j):
    a_i, b_i = e_i
    a_j, b_j = e_j
    return a_j * a_i, a_j * b_i + b_j


def s5_discretize(a_re, a_im, log_step, b_re, b_im, c_re, c_im):
    a = lax.complex(a_re.astype(jnp.float32), a_im.astype(jnp.float32))
    dt = jnp.exp(log_step.astype(jnp.float32))[..., None]
    abar = jnp.exp(a * dt)
    bmat = lax.complex(b_re.astype(jnp.float32), b_im.astype(jnp.float32))
    bbar = ((abar - 1.0) / a)[..., None] * bmat
    cmat = lax.complex(c_re.astype(jnp.float32), c_im.astype(jnp.float32))
    return abar, bbar, cmat


def s5_bidirectional(u, abar, bbar, cmat, d_skip):
    B, S, _ = u.shape
    nb = S5_GROUPS // S5_GROUPS_PER_BLOCK
    uf = u.astype(jnp.float32)
    ub = uf.reshape(B, S, nb, S5_GROUPS_PER_BLOCK, S5_P).transpose(2, 0, 1, 3, 4)

    def split_blocks(t):
        return t.reshape((2, nb, S5_GROUPS_PER_BLOCK) + t.shape[2:]).swapaxes(0, 1)

    def block(args):
        ublk, ab, bb, cb = args
        bu = jnp.einsum('dgnp,bsgp->dbsgn', bb, ublk.astype(jnp.complex64))
        a_full = jnp.broadcast_to(ab[:, None, None, :, :], bu.shape)
        h_f = lax.associative_scan(_ssm_combine, (a_full[0], bu[0]), axis=1)[1]
        h_b = lax.associative_scan(_ssm_combine, (a_full[1], bu[1]), axis=1, reverse=True)[1]
        y = jnp.einsum('gpn,bsgn->bsgp', cb[0], h_f) + jnp.einsum('gpn,bsgn->bsgp', cb[1], h_b)
        return jnp.real(y)

    y = lax.map(block, (ub, split_blocks(abar), split_blocks(bbar), split_blocks(cmat)))
    y = y.transpose(1, 2, 0, 3, 4).reshape(B, S, D_MODEL)
    return y + d_skip.astype(jnp.float32) * uf


def swiglu(h, w_gu, w_d):
    g, u = jnp.split(h @ w_gu, 2, axis=-1)
    return (jax.nn.silu(g) * u) @ w_d


def moe_swiglu(h, w_router, w_gu, w_d):
    B, S, D = h.shape
    hf = h.reshape(B * S, D)
    logits = (hf @ w_router).astype(jnp.float32)
    top_v, top_i = lax.top_k(logits, TOP_K)
    gates = jax.nn.softmax(top_v, axis=-1)
    comb = jnp.sum(jax.nn.one_hot(top_i, N_EXPERTS, dtype=jnp.float32) * gates[..., None], axis=1)
    comb = comb.astype(h.dtype)
    out = jnp.zeros_like(hf)
    for e in range(N_EXPERTS):
        out = out + comb[:, e:e + 1] * swiglu(hf, w_gu[e], w_d[e])
    return out.reshape(B, S, D)


def trunk(x, g_mix_e, w_in_e, lambda_qk, subln_g, conv_w, w_out_e, g_ffn_e, w_gate_up, w_down,
          g_mix_o, s5_a_re, s5_a_im, s5_log_step, s5_b_re, s5_b_im, s5_c_re, s5_c_im, s5_d,
          w_glu, g_ffn_o, w_router, w_exp_gate_up, w_exp_down, g_final):
    B, S, _ = x.shape
    cos, sin = rope_tables(S)
    splits = [ATTN_WIDTH, 2 * ATTN_WIDTH, 3 * ATTN_WIDTH, 3 * ATTN_WIDTH + CONV_CH,
              3 * ATTN_WIDTH + 2 * CONV_CH]
    for layer in range(DEPTH):
        i = layer // 2
        if layer % 2 == 0:
            lambda_init = 0.8 - 0.6 * math.exp(-0.3 * layer)
            h = rms_norm(x, g_mix_e[i])
            z = h @ w_in_e[i]
            q, k, v, bg, cg, xc = jnp.split(z, splits, axis=-1)
            q = apply_partial_rope(q.reshape(B, S, N_DIFF_HEADS, 2, QK_DIM), cos, sin)
            k = apply_partial_rope(k.reshape(B, S, N_DIFF_HEADS, 2, QK_DIM), cos, sin)
            v = v.reshape(B, S, N_DIFF_HEADS, V_DIM)
            lq = lambda_qk[i].astype(jnp.float32)
            lam = jnp.exp(jnp.sum(lq[0] * lq[1])) - jnp.exp(jnp.sum(lq[2] * lq[3])) + lambda_init
            a_out = diff_attention(q, k, v, lam, subln_g[i], lambda_init)
            c_out = short_gated_conv(bg, cg, xc, conv_w[i])
            x = x + jnp.concatenate([a_out, c_out], axis=-1) @ w_out_e[i]
            x = x + swiglu(rms_norm(x, g_ffn_e[i]), w_gate_up[i], w_down[i])
        else:
            h = rms_norm(x, g_mix_o[i])
            abar, bbar, cmat = s5_discretize(s5_a_re[i], s5_a_im[i], s5_log_step[i],
                                             s5_b_re[i], s5_b_im[i], s5_c_re[i], s5_c_im[i])
            y = s5_bidirectional(h, abar, bbar, cmat, s5_d[i]).astype(x.dtype)
            ga, gb = jnp.split(jax.nn.gelu(y) @ w_glu[i], 2, axis=-1)
            x = x + ga * jax.nn.sigmoid(gb)
            x = x + moe_swiglu(rms_norm(x, g_ffn_o[i]), w_router[i], w_exp_gate_up[i], w_exp_down[i])
    return rms_norm(x, g_final)


def setup_inputs(seed: int = 0) -> dict:
    key = jax.random.key(seed)
    ks = jax.random.split(key, 32)
    f32 = jnp.float32
    nrm = lambda k, shape, s: jax.random.normal(k, shape, f32) * s
    gain = lambda k, shape: 1.0 + 0.02 * jax.random.normal(k, shape, f32)
    n_idx = jnp.arange(S5_N, dtype=f32)
    a_shape = (N_ODD, 2, S5_GROUPS, S5_N)
    return {
        "x_prompt": jax.random.normal(ks[0], (BATCH, SEQ, D_MODEL), f32),
        "x_sample": jax.random.normal(ks[1], (DEC_BATCH, DEC_SEQ, D_MODEL), f32),
        "g_mix_e": gain(ks[2], (N_EVEN, D_MODEL)),
        "w_in_e": nrm(ks[3], (N_EVEN, D_MODEL, IN_COLS), D_MODEL ** -0.5),
        "lambda_qk": nrm(ks[4], (N_EVEN, 4, QK_DIM), 0.1),
        "subln_g": gain(ks[5], (N_EVEN, V_DIM)),
        "conv_w": nrm(ks[6], (N_EVEN, CONV_K, CONV_CH), CONV_K ** -0.5),
        "w_out_e": nrm(ks[7], (N_EVEN, D_MODEL, D_MODEL), D_MODEL ** -0.5),
        "g_ffn_e": gain(ks[8], (N_EVEN, D_MODEL)),
        "w_gate_up": nrm(ks[9], (N_EVEN, D_MODEL, 2 * FF_DENSE), D_MODEL ** -0.5),
        "w_down": nrm(ks[10], (N_EVEN, FF_DENSE, D_MODEL), FF_DENSE ** -0.5),
        "g_mix_o": gain(ks[11], (N_ODD, D_MODEL)),
        "s5_a_re": -0.5 + 0.01 * jax.random.normal(ks[12], a_shape, f32),
        "s5_a_im": math.pi * n_idx + 0.01 * jax.random.normal(ks[13], a_shape, f32),
        "s5_log_step": jax.random.uniform(ks[14], (N_ODD, 2, S5_GROUPS), f32,
                                          math.log(0.001), math.log(0.1)),
        "s5_b_re": nrm(ks[15], (N_ODD, 2, S5_GROUPS, S5_N, S5_P), (2 * S5_P) ** -0.5),
        "s5_b_im": nrm(ks[16], (N_ODD, 2, S5_GROUPS, S5_N, S5_P), (2 * S5_P) ** -0.5),
        "s5_c_re": nrm(ks[17], (N_ODD, 2, S5_GROUPS, S5_P, S5_N), (2 * S5_N) ** -0.5),
        "s5_c_im": nrm(ks[18], (N_ODD, 2, S5_GROUPS, S5_P, S5_N), (2 * S5_N) ** -0.5),
        "s5_d": nrm(ks[19], (N_ODD, D_MODEL), 1.0),
        "w_glu": nrm(ks[20], (N_ODD, D_MODEL, 2 * D_MODEL), D_MODEL ** -0.5),
        "g_ffn_o": gain(ks[21], (N_ODD, D_MODEL)),
        "w_router": nrm(ks[22], (N_ODD, D_MODEL, N_EXPERTS), D_MODEL ** -0.5),
        "w_exp_gate_up": nrm(ks[23], (N_ODD, N_EXPERTS, D_MODEL, 2 * FF_EXPERT), D_MODEL ** -0.5),
        "w_exp_down": nrm(ks[24], (N_ODD, N_EXPERTS, FF_EXPERT, D_MODEL), FF_EXPERT ** -0.5),
        "g_final": gain(ks[25], (D_MODEL,)),
    }


def reference(x_prompt, x_sample, g_mix_e, w_in_e, lambda_qk, subln_g, conv_w, w_out_e, g_ffn_e,
              w_gate_up, w_down, g_mix_o, s5_a_re, s5_a_im, s5_log_step, s5_b_re, s5_b_im,
              s5_c_re, s5_c_im, s5_d, w_glu, g_ffn_o, w_router, w_exp_gate_up, w_exp_down, g_final):
    y_prompt = trunk(x_prompt, g_mix_e, w_in_e, lambda_qk, subln_g, conv_w, w_out_e, g_ffn_e,
                     w_gate_up, w_down, g_mix_o, s5_a_re, s5_a_im, s5_log_step, s5_b_re, s5_b_im,
                     s5_c_re, s5_c_im, s5_d, w_glu, g_ffn_o, w_router, w_exp_gate_up, w_exp_down,
                     g_final)
    y_sample = trunk(x_sample, g_mix_e, w_in_e, lambda_qk, subln_g, conv_w, w_out_e, g_ffn_e,
                     w_gate_up, w_down, g_mix_o, s5_a_re, s5_a_im, s5_log_step, s5_b_re, s5_b_im,
                     s5_c_re, s5_c_im, s5_d, w_glu, g_ffn_o, w_router, w_exp_gate_up, w_exp_down,
                     g_final)
    return (y_prompt, y_sample)
```

```python
import functools
import math

import jax
import jax.numpy as jnp
from jax import lax
from jax.experimental import pallas as pl
from jax.experimental.pallas import tpu as pltpu

F32 = jnp.float32
BF16 = jnp.bfloat16

D_MODEL = 2048
ATTN_WIDTH = D_MODEL // 2
CONV_CH = D_MODEL - ATTN_WIDTH
N_HEADS = 8
QK_DIM = 64
V_DIM = 2 * QK_DIM
ROPE_DIM = QK_DIM // 4
ROPE_THETA = 500000.0
IN_COLS = 3 * ATTN_WIDTH + 3 * CONV_CH
S5_P = 16
S5_GROUPS = D_MODEL // S5_P
S5_N = 64
FF_DENSE = 5632
N_EXPERTS = 8
FF_EXPERT = 7168
RMS_EPS = 1e-5
LAMBDA_INIT_0 = 0.8 - 0.6 * math.exp(-0.3 * 0)

LANES = 128
SUBLANES = 8
S5_CHUNK = 64
S5_ROW = S5_CHUNK * S5_P
VMEM_LIMIT = 52 * 1024 * 1024


def _params(sem):
    return pltpu.CompilerParams(dimension_semantics=sem, vmem_limit_bytes=VMEM_LIMIT)


def _rms(x, g):
    ms = jnp.mean(x * x, axis=-1, keepdims=True)
    return x * lax.rsqrt(ms + RMS_EPS) * g


def _inproj_kernel(x_ref, g_ref, w_ref, ra_ref, rm_ref, rp_ref, o_ref, h_ref, *, tn, n_q, n_rope):
    j = pl.program_id(1)

    @pl.when(j == 0)
    def _():
        h_ref[...] = _rms(x_ref[...], g_ref[...]).astype(BF16)

    acc = jnp.dot(h_ref[...], w_ref[...], preferred_element_type=F32)

    @pl.when(j < n_rope)
    def _():
        scale = jnp.where(j < n_q, QK_DIM ** -0.5, 1.0).astype(F32)
        a = ra_ref[...] * scale
        bm = rm_ref[...] * scale
        bp = rp_ref[...] * scale
        for c in range(tn // LANES):
            t = acc[:, c * LANES:(c + 1) * LANES]
            r = t * a + pltpu.roll(t, LANES - ROPE_DIM // 2, 1) * bm + pltpu.roll(t, ROPE_DIM // 2, 1) * bp
            o_ref[:, c * LANES:(c + 1) * LANES] = r.astype(o_ref.dtype)

    @pl.when(j >= n_rope)
    def _():
        o_ref[...] = acc.astype(o_ref.dtype)


def _rope_tables(seq):
    half = ROPE_DIM // 2
    inv = 1.0 / (ROPE_THETA ** (jnp.arange(0, ROPE_DIM, 2, dtype=F32) / ROPE_DIM))
    ang = jnp.arange(seq, dtype=F32)[:, None] * inv[None, :]
    cos, sin = jnp.cos(ang), jnp.sin(ang)
    ones = jnp.ones((seq, QK_DIM - ROPE_DIM), F32)
    zeros = jnp.zeros((seq, QK_DIM - ROPE_DIM), F32)
    zh = jnp.zeros((seq, half), F32)
    a = jnp.concatenate([cos, cos, ones], axis=-1)
    bm = jnp.concatenate([-sin, zh, zeros], axis=-1)
    bp = jnp.concatenate([zh, sin, zeros], axis=-1)
    rep = LANES // QK_DIM
    return jnp.tile(a, (1, rep)), jnp.tile(bm, (1, rep)), jnp.tile(bp, (1, rep))


def _in_proj(x2, g, w_bf, seq):
    t, d = x2.shape
    tm = min(1024, seq)
    tn = 1024
    ra, rm, rp = _rope_tables(seq)
    nblk = seq // tm
    rope_spec = pl.BlockSpec((tm, LANES), lambda i, j: (i % nblk, 0))
    return pl.pallas_call(
        functools.partial(_inproj_kernel, tn=tn, n_q=ATTN_WIDTH // tn, n_rope=2 * ATTN_WIDTH // tn),
        grid=(t // tm, IN_COLS // tn),
        in_specs=[
            pl.BlockSpec((tm, d), lambda i, j: (i, 0)),
            pl.BlockSpec((1, d), lambda i, j: (0, 0)),
            pl.BlockSpec((d, tn), lambda i, j: (0, j)),
            rope_spec, rope_spec, rope_spec,
        ],
        out_specs=pl.BlockSpec((tm, tn), lambda i, j: (i, j)),
        out_shape=jax.ShapeDtypeStruct((t, IN_COLS), BF16),
        scratch_shapes=[pltpu.VMEM((tm, d), BF16)],
        compiler_params=_params(("parallel", "arbitrary")),
        name="in_proj",
    )(x2, g.reshape(1, d), w_bf, ra, rm, rp)


def _attn_kernel(q_ref, k_ref, v_ref, lq_ref, sg_ref, o_ref, qq_ref, m_ref, l_ref, acc_ref, *, tq, tk, nk):
    q = q_ref[...]
    lane = lax.broadcasted_iota(jnp.int32, (tq, LANES), 1)
    zero = jnp.zeros_like(q)
    qq_ref[0:tq, :] = jnp.where(lane < QK_DIM, q, zero)
    qq_ref[tq:2 * tq, :] = jnp.where(lane >= QK_DIM, q, zero)
    m_ref[...] = jnp.full(m_ref.shape, -jnp.inf, F32)
    l_ref[...] = jnp.zeros(l_ref.shape, F32)
    acc_ref[...] = jnp.zeros(acc_ref.shape, F32)

    def body(kk, carry):
        start = pl.multiple_of(kk * tk, tk)
        ks = k_ref[pl.ds(start, tk), :]
        vs = v_ref[pl.ds(start, tk), :]
        s = lax.dot_general(qq_ref[...], ks, (((1,), (1,)), ((), ())), preferred_element_type=F32)
        m_prev = m_ref[...]
        m_new = jnp.maximum(m_prev, jnp.max(s, axis=-1, keepdims=True))
        alpha = jnp.exp(m_prev - m_new)
        p = jnp.exp(s - m_new)
        l_ref[...] = alpha * l_ref[...] + jnp.sum(p, axis=-1, keepdims=True)
        acc_ref[...] = alpha * acc_ref[...] + jnp.dot(p.astype(BF16), vs, preferred_element_type=F32)
        m_ref[...] = m_new
        return carry

    lax.fori_loop(0, nk, body, 0)

    lq = lq_ref[...]
    lam = (jnp.exp(jnp.sum(lq[0:1, :] * lq[1:2, :], axis=-1, keepdims=True))
           - jnp.exp(jnp.sum(lq[2:3, :] * lq[3:4, :], axis=-1, keepdims=True)) + LAMBDA_INIT_0)
    o = acc_ref[...] / l_ref[...]
    diff = o[0:tq, :] - lam * o[tq:2 * tq, :]
    o_ref[...] = (_rms(diff, sg_ref[...]) * (1.0 - LAMBDA_INIT_0)).astype(o_ref.dtype)


def _attention(z3, lambda_qk, subln_g):
    b, s, _ = z3.shape
    tq = min(512, s)
    tk = min(512, s)
    kcol = ATTN_WIDTH // V_DIM
    return pl.pallas_call(
        functools.partial(_attn_kernel, tq=tq, tk=tk, nk=s // tk),
        grid=(b, N_HEADS, s // tq),
        in_specs=[
            pl.BlockSpec((None, tq, V_DIM), lambda bi, h, i: (bi, i, h)),
            pl.BlockSpec((None, s, V_DIM), lambda bi, h, i: (bi, 0, kcol + h)),
            pl.BlockSpec((None, s, V_DIM), lambda bi, h, i: (bi, 0, 2 * kcol + h)),
            pl.BlockSpec((4, QK_DIM), lambda bi, h, i: (0, 0)),
            pl.BlockSpec((1, V_DIM), lambda bi, h, i: (0, 0)),
        ],
        out_specs=pl.BlockSpec((None, tq, V_DIM), lambda bi, h, i: (bi, i, h)),
        out_shape=jax.ShapeDtypeStruct((b, s, ATTN_WIDTH), BF16),
        scratch_shapes=[
            pltpu.VMEM((2 * tq, V_DIM), BF16),
            pltpu.VMEM((2 * tq, 1), F32),
            pltpu.VMEM((2 * tq, 1), F32),
            pltpu.VMEM((2 * tq, V_DIM), F32),
        ],
        compiler_params=_params(("parallel", "parallel", "arbitrary")),
        name="diff_attention",
    )(z3, z3, z3, lambda_qk.astype(F32), subln_g.reshape(1, V_DIM).astype(F32))


def _outproj_kernel(a_ref, bg_ref, cg_ref, xc_ref, cgp_ref, xcp_ref, cgn_ref, xcn_ref, cw_ref, x_ref, w_ref,
                    o_ref, *, tm, nt):
    i = pl.program_id(1)
    u = cg_ref[...].astype(F32) * xc_ref[...].astype(F32)
    u_before = cgp_ref[SUBLANES - 1:SUBLANES, :].astype(F32) * xcp_ref[SUBLANES - 1:SUBLANES, :].astype(F32)
    u_after = cgn_ref[0:1, :].astype(F32) * xcn_ref[0:1, :].astype(F32)
    u_before = u_before * (i > 0).astype(F32)
    u_after = u_after * (i < nt - 1).astype(F32)
    row = lax.broadcasted_iota(jnp.int32, u.shape, 0)
    u_prev = jnp.where(row == 0, u_before, pltpu.roll(u, 1, 0))
    u_next = jnp.where(row == tm - 1, u_after, pltpu.roll(u, tm - 1, 0))
    cw = cw_ref[...]
    y = u_prev * cw[0:1, :] + u * cw[1:2, :] + u_next * cw[2:3, :]
    c = (bg_ref[...].astype(F32) * y).astype(BF16)
    acc = jnp.dot(a_ref[...], w_ref[0:ATTN_WIDTH, :], preferred_element_type=F32)
    acc = acc + jnp.dot(c, w_ref[ATTN_WIDTH:D_MODEL, :], preferred_element_type=F32)
    o_ref[...] = x_ref[...] + acc


def _out_proj(a3, z3, conv_w, x3, w_bf):
    b, s, _ = z3.shape
    tm = min(512, s)
    nt = s // tm
    rb = tm // SUBLANES
    last = s // SUBLANES - 1
    cb = 3 * ATTN_WIDTH // CONV_CH
    main = lambda col: pl.BlockSpec((None, tm, CONV_CH), lambda bi, i: (bi, i, col))
    prev = lambda col: pl.BlockSpec((None, SUBLANES, CONV_CH), lambda bi, i: (bi, jnp.maximum(i * rb - 1, 0), col))
    nxt = lambda col: pl.BlockSpec((None, SUBLANES, CONV_CH), lambda bi, i: (bi, jnp.minimum((i + 1) * rb, last), col))
    return pl.pallas_call(
        functools.partial(_outproj_kernel, tm=tm, nt=nt),
        grid=(b, nt),
        in_specs=[
            pl.BlockSpec((None, tm, ATTN_WIDTH), lambda bi, i: (bi, i, 0)),
            main(cb), main(cb + 1), main(cb + 2),
            prev(cb + 1), prev(cb + 2), nxt(cb + 1), nxt(cb + 2),
            pl.BlockSpec((3, CONV_CH), lambda bi, i: (0, 0)),
            pl.BlockSpec((None, tm, D_MODEL), lambda bi, i: (bi, i, 0)),
            pl.BlockSpec((D_MODEL, D_MODEL), lambda bi, i: (0, 0)),
        ],
        out_specs=pl.BlockSpec((None, tm, D_MODEL), lambda bi, i: (bi, i, 0)),
        out_shape=jax.ShapeDtypeStruct((b, s, D_MODEL), F32),
        compiler_params=_params(("parallel", "arbitrary")),
        name="out_proj_conv",
    )(a3, z3, z3, z3, z3, z3, z3, z3, conv_w.astype(F32), x3, w_bf)


def _ffn_kernel(x_ref, g_ref, wg_ref, wu_ref, wd_ref, o_ref, h_ref):
    f = pl.program_id(1)

    @pl.when(f == 0)
    def _():
        x = x_ref[...]
        h_ref[...] = _rms(x, g_ref[...]).astype(BF16)
        o_ref[...] = x

    h = h_ref[...]
    gate = jnp.dot(h, wg_ref[...], preferred_element_type=F32)
    up = jnp.dot(h, wu_ref[...], preferred_element_type=F32)
    act = (gate * jax.nn.sigmoid(gate) * up).astype(BF16)
    o_ref[...] += jnp.dot(act, wd_ref[...], preferred_element_type=F32)


def _ffn(x2, g, wgu_bf, wd_bf):
    t, d = x2.shape
    tm = min(512, t)
    tf = 512
    nf = FF_DENSE // tf
    return pl.pallas_call(
        _ffn_kernel,
        grid=(t // tm, nf),
        in_specs=[
            pl.BlockSpec((tm, d), lambda i, f: (i, 0)),
            pl.BlockSpec((1, d), lambda i, f: (0, 0)),
            pl.BlockSpec((d, tf), lambda i, f: (0, f)),
            pl.BlockSpec((d, tf), lambda i, f: (0, f + nf)),
            pl.BlockSpec((tf, d), lambda i, f: (f, 0)),
        ],
        out_specs=pl.BlockSpec((tm, d), lambda i, f: (i, 0)),
        out_shape=jax.ShapeDtypeStruct((t, d), F32),
        scratch_shapes=[pltpu.VMEM((tm, d), BF16)],
        compiler_params=_params(("parallel", "arbitrary")),
        name="ffn_swiglu",
    )(x2, g.reshape(1, d), wgu_bf, wgu_bf, wd_bf)


def _norm_kernel(x_ref, g_ref, o_ref):
    o_ref[...] = _rms(x_ref[...], g_ref[...]).astype(o_ref.dtype)


def _norm_bf16(x2, g):
    t, d = x2.shape
    tm = min(1024, t)
    return pl.pallas_call(
        _norm_kernel,
        grid=(t // tm,),
        in_specs=[pl.BlockSpec((tm, d), lambda i: (i, 0)), pl.BlockSpec((1, d), lambda i: (0, 0))],
        out_specs=pl.BlockSpec((tm, d), lambda i: (i, 0)),
        out_shape=jax.ShapeDtypeStruct((t, d), BF16),
        compiler_params=_params(("parallel",)),
        name="s5_norm",
    )(x2, g.reshape(1, d))


_K_BP, _K_CP, _K_BW, _K_CS = 0, 1, 2, 3
_N_KIND = 4
_SCAN_STEPS_MAX = 16


def _s5_kernel(u_ref, pt_ref, yt_ref, al_ref, pos_ref, rem_ref, o_ref, tab_ref, *, rows, n_steps):
    def build(s, carry):
        for d in range(2):
            for kind in range(_N_KIND):
                x = pt_ref[d * 2 * _N_KIND + 2 * kind, pl.ds(s, 1), :]
                xs = pt_ref[d * 2 * _N_KIND + 2 * kind + 1, pl.ds(s, 1), :]
                par = 0 if kind in (_K_BP, _K_BW) else 2
                ya = yt_ref[d * 4 + par]
                yb = yt_ref[d * 4 + par + 1]
                val = x * ya + xs * yb
                tab_ref[d * _N_KIND + kind, pl.ds(pl.multiple_of(s * S5_P, S5_P), S5_P), :] = val.astype(BF16)
        return carry

    lax.fori_loop(0, S5_CHUNK, build, 0)

    nt = (((1,), (1,)), ((), ()))
    ri = lax.broadcasted_iota(jnp.int32, (S5_ROW, S5_ROW), 0) // S5_P
    ci = lax.broadcasted_iota(jnp.int32, (S5_ROW, S5_ROW), 1) // S5_P
    t_f = lax.dot_general(tab_ref[_K_BP], tab_ref[_K_CP], nt, preferred_element_type=F32)
    t_b = lax.dot_general(tab_ref[_N_KIND + _K_BP], tab_ref[_N_KIND + _K_CP], nt, preferred_element_type=F32)
    toep = (jnp.where(ri <= ci, t_f, 0.0) + jnp.where(ri >= ci, t_b, 0.0)).astype(BF16)

    u = u_ref[...]
    y = jnp.dot(u, toep, preferred_element_type=F32)

    lane = lax.broadcasted_iota(jnp.int32, (rows, LANES), 1)
    for d in range(2):
        fwd = d == 0
        dist = pos_ref[...] if fwd else rem_ref[...]
        st = jnp.dot(u, tab_ref[d * _N_KIND + _K_BW], preferred_element_type=F32)
        for k in range(n_steps):
            step = 1 << k
            sh = pltpu.roll(st, step if fwd else rows - step, 0)
            sh = jnp.where(dist >= step, sh, 0.0)
            shs = pltpu.roll(sh, S5_N, 1)
            pa = al_ref[d * 2 * _SCAN_STEPS_MAX + k:d * 2 * _SCAN_STEPS_MAX + k + 1, :]
            pb = al_ref[d * 2 * _SCAN_STEPS_MAX + _SCAN_STEPS_MAX + k:d * 2 * _SCAN_STEPS_MAX + _SCAN_STEPS_MAX + k + 1, :]
            st = st + sh * pa + shs * pb
        inc = pltpu.roll(st, 1 if fwd else rows - 1, 0)
        inc = jnp.where(dist >= 1, inc, 0.0).astype(BF16)
        y = y + lax.dot_general(inc, tab_ref[d * _N_KIND + _K_CS], nt, preferred_element_type=F32)
    del lane
    o_ref[...] = y.astype(o_ref.dtype)


def _s5_tables(a_re, a_im, log_step, b_re, b_im, c_re, c_im, n_steps):
    dt = jnp.exp(log_step.astype(F32))[..., None]
    a_re = a_re.astype(F32)
    a_im = a_im.astype(F32)
    lr, li = a_re * dt, a_im * dt
    er = jnp.exp(lr)
    abr, abi = er * jnp.cos(li), er * jnp.sin(li)
    den = a_re * a_re + a_im * a_im
    fr = ((abr - 1.0) * a_re + abi * a_im) / den
    fi = (abi * a_re - (abr - 1.0) * a_im) / den
    b_re = b_re.astype(F32)
    b_im = b_im.astype(F32)
    bbr = fr[..., None] * b_re - fi[..., None] * b_im
    bbi = fr[..., None] * b_im + fi[..., None] * b_re
    bbr, bbi = jnp.swapaxes(bbr, -1, -2), jnp.swapaxes(bbi, -1, -2)
    cr, ci = c_re.astype(F32), c_im.astype(F32)
    cat = lambda lo, hi: jnp.concatenate([lo, hi], axis=-1)
    yt = jnp.stack([cat(bbr, bbr), cat(-bbi, bbi), cat(cr, -cr), cat(-ci, -ci)], axis=2)
    yt = jnp.swapaxes(yt, 0, 1).reshape(S5_GROUPS, 2 * 4, S5_P, LANES)

    idx = jnp.arange(S5_CHUNK, dtype=F32)
    length = float(S5_CHUNK)
    expo = jnp.stack([
        jnp.stack([-idx, idx, length - 1.0 - idx, idx + 1.0]),
        jnp.stack([idx, -idx, idx, length - idx]),
    ])
    e = expo[:, None, :, :, None]
    mag = jnp.exp(lr[:, :, None, None, :] * e)
    ang = li[:, :, None, None, :] * e
    pr, pi = mag * jnp.cos(ang), mag * jnp.sin(ang)
    pt = jnp.stack([cat(pr, pi), cat(pi, pr)], axis=3)
    pt = jnp.swapaxes(pt, 0, 1).reshape(S5_GROUPS, 2 * _N_KIND * 2, S5_CHUNK, LANES)

    steps = length * (2.0 ** jnp.arange(_SCAN_STEPS_MAX, dtype=F32))
    se = steps[None, None, :, None]
    smag = jnp.exp(lr[:, :, None, :] * se)
    sang = li[:, :, None, :] * se
    sr, si = smag * jnp.cos(sang), smag * jnp.sin(sang)
    al = jnp.concatenate([cat(sr, sr), cat(-si, si)], axis=2)
    al = jnp.swapaxes(al, 0, 1).reshape(S5_GROUPS, 2 * 2 * _SCAN_STEPS_MAX, LANES)
    del n_steps
    return pt, yt, al


def _s5(u_list, shapes, a_re, a_im, log_step, b_re, b_im, c_re, c_im):
    pos_parts, rem_parts, max_cps = [], [], 1
    for (b, s) in shapes:
        cps = s // S5_CHUNK
        max_cps = max(max_cps, cps)
        p = jnp.tile(jnp.arange(cps, dtype=jnp.int32), b)
        pos_parts.append(p)
        rem_parts.append(cps - 1 - p)
    n_steps = max(1, (max_cps - 1).bit_length())
    pos = jnp.concatenate(pos_parts)
    rem = jnp.concatenate(rem_parts)
    rows = pos.shape[0]
    pos = jnp.broadcast_to(pos[:, None], (rows, LANES))
    rem = jnp.broadcast_to(rem[:, None], (rows, LANES))
    pt, yt, al = _s5_tables(a_re, a_im, log_step, b_re, b_im, c_re, c_im, n_steps)

    u_all = jnp.concatenate(u_list, axis=0)
    u_t = u_all.reshape(rows, S5_CHUNK, S5_GROUPS, S5_P).transpose(2, 0, 1, 3).reshape(S5_GROUPS, rows, S5_ROW)
    y_t = pl.pallas_call(
        functools.partial(_s5_kernel, rows=rows, n_steps=n_steps),
        grid=(S5_GROUPS,),
        in_specs=[
            pl.BlockSpec((None, rows, S5_ROW), lambda g: (g, 0, 0)),
            pl.BlockSpec((None, 2 * _N_KIND * 2, S5_CHUNK, LANES), lambda g: (g, 0, 0, 0)),
            pl.BlockSpec((None, 2 * 4, S5_P, LANES), lambda g: (g, 0, 0, 0)),
            pl.BlockSpec((None, 2 * 2 * _SCAN_STEPS_MAX, LANES), lambda g: (g, 0, 0)),
            pl.BlockSpec((rows, LANES), lambda g: (0, 0)),
            pl.BlockSpec((rows, LANES), lambda g: (0, 0)),
        ],
        out_specs=pl.BlockSpec((None, rows, S5_ROW), lambda g: (g, 0, 0)),
        out_shape=jax.ShapeDtypeStruct((S5_GROUPS, rows, S5_ROW), BF16),
        scratch_shapes=[pltpu.VMEM((2 * _N_KIND, S5_ROW, LANES), BF16)],
        compiler_params=_params(("parallel",)),
        name="s5_chunked",
    )(u_t, pt, yt, al, pos, rem)
    y_all = y_t.reshape(S5_GROUPS, rows, S5_CHUNK, S5_P).transpose(1, 2, 0, 3).reshape(rows * S5_CHUNK, D_MODEL)
    outs, off = [], 0
    for (b, s) in shapes:
        outs.append(y_all[off:off + b * s])
        off += b * s
    return outs


def _glu_kernel(x_ref, y_ref, g_ref, d_ref, wa_ref, wb_ref, o_ref, a_ref, *, tn):
    j = pl.program_id(1)

    @pl.when(j == 0)
    def _():
        h = _rms(x_ref[...], g_ref[...])
        t = (y_ref[...].astype(F32) + d_ref[...] * h)
        gelu = 0.5 * t * (1.0 + jnp.tanh(math.sqrt(2.0 / math.pi) * (t + 0.044715 * (t * t * t))))
        a_ref[...] = gelu.astype(BF16)

    a = a_ref[...]
    ga = jnp.dot(a, wa_ref[...], preferred_element_type=F32)
    gb = jnp.dot(a, wb_ref[...], preferred_element_type=F32)
    xs = x_ref[:, pl.ds(pl.multiple_of(j * tn, tn), tn)]
    o_ref[...] = xs + ga * jax.nn.sigmoid(gb)


def _glu(x2, y2, g, dskip, w_bf):
    t, d = x2.shape
    tm = min(512, t)
    tn = 1024
    nj = d // tn
    return pl.pallas_call(
        functools.partial(_glu_kernel, tn=tn),
        grid=(t // tm, nj),
        in_specs=[
            pl.BlockSpec((tm, d), lambda i, j: (i, 0)),
            pl.BlockSpec((tm, d), lambda i, j: (i, 0)),
            pl.BlockSpec((1, d), lambda i, j: (0, 0)),
            pl.BlockSpec((1, d), lambda i, j: (0, 0)),
            pl.BlockSpec((d, tn), lambda i, j: (0, j)),
            pl.BlockSpec((d, tn), lambda i, j: (0, j + nj)),
        ],
        out_specs=pl.BlockSpec((tm, tn), lambda i, j: (i, j)),
        out_shape=jax.ShapeDtypeStruct((t, d), F32),
        scratch_shapes=[pltpu.VMEM((tm, d), BF16)],
        compiler_params=_params(("parallel", "arbitrary")),
        name="s5_glu",
    )(x2, y2, g.reshape(1, d), dskip.reshape(1, d).astype(F32), w_bf, w_bf)


def _moe_kernel(x_ref, g_ref, wr_ref, wg_ref, wu_ref, wd_ref, gf_ref, o_ref, h_ref, comb_ref, *, nf):
    e = pl.program_id(1)
    f = pl.program_id(2)

    @pl.when((e == 0) & (f == 0))
    def _():
        x = x_ref[...]
        h = _rms(x, g_ref[...])
        h_ref[...] = h.astype(BF16)
        o_ref[...] = x
        logits = jnp.dot(h, wr_ref[...], preferred_element_type=F32, precision=lax.Precision.HIGHEST)
        lane = lax.broadcasted_iota(jnp.int32, logits.shape, 1)
        neg = jnp.float32(-jnp.inf)
        lg = jnp.where(lane < N_EXPERTS, logits, neg)
        m1 = jnp.max(lg, axis=-1, keepdims=True)
        i1 = jnp.min(jnp.where(lg == m1, lane, LANES), axis=-1, keepdims=True)
        lg2 = jnp.where(lane == i1, neg, lg)
        m2 = jnp.max(lg2, axis=-1, keepdims=True)
        i2 = jnp.min(jnp.where(lg2 == m2, lane, LANES), axis=-1, keepdims=True)
        ex = jnp.exp(m2 - m1)
        den = 1.0 + ex
        comb_ref[...] = jnp.where(lane == i1, 1.0 / den, 0.0) + jnp.where(lane == i2, ex / den, 0.0)

    h = h_ref[...]
    gate = jnp.dot(h, wg_ref[...], preferred_element_type=F32)
    up = jnp.dot(h, wu_ref[...], preferred_element_type=F32)
    comb = comb_ref[...]
    lane = lax.broadcasted_iota(jnp.int32, comb.shape, 1)
    ce = jnp.sum(jnp.where(lane == e, comb, 0.0), axis=-1, keepdims=True)
    act = (gate * jax.nn.sigmoid(gate) * up * ce).astype(BF16)
    o_ref[...] += jnp.dot(act, wd_ref[...], preferred_element_type=F32)

    @pl.when((e == N_EXPERTS - 1) & (f == nf - 1))
    def _():
        o_ref[...] = _rms(o_ref[...], gf_ref[...])


def _moe(x2, g, w_router, wgu_bf, wd_bf, g_final):
    t, d = x2.shape
    tm = min(512, t)
    tf = 512
    nf = FF_EXPERT // tf
    wr = jnp.zeros((d, LANES), F32).at[:, :N_EXPERTS].set(w_router.astype(F32))
    return pl.pallas_call(
        functools.partial(_moe_kernel, nf=nf),
        grid=(t // tm, N_EXPERTS, nf),
        in_specs=[
            pl.BlockSpec((tm, d), lambda i, e, f: (i, 0)),
            pl.BlockSpec((1, d), lambda i, e, f: (0, 0)),
            pl.BlockSpec((d, LANES), lambda i, e, f: (0, 0)),
            pl.BlockSpec((None, d, tf), lambda i, e, f: (e, 0, f)),
            pl.BlockSpec((None, d, tf), lambda i, e, f: (e, 0, f + nf)),
            pl.BlockSpec((None, tf, d), lambda i, e, f: (e, f, 0)),
            pl.BlockSpec((1, d), lambda i, e, f: (0, 0)),
        ],
        out_specs=pl.BlockSpec((tm, d), lambda i, e, f: (i, 0)),
        out_shape=jax.ShapeDtypeStruct((t, d), F32),
        scratch_shapes=[pltpu.VMEM((tm, d), BF16), pltpu.VMEM((tm, LANES), F32)],
        compiler_params=_params(("parallel", "arbitrary", "arbitrary")),
        name="moe_dense",
    )(x2, g.reshape(1, d), wr, wgu_bf, wgu_bf, wd_bf, g_final.reshape(1, d))


def _trunks(xs, g_mix_e, w_in_e, lambda_qk, subln_g, conv_w, w_out_e, g_ffn_e, w_gate_up, w_down,
            g_mix_o, s5_a_re, s5_a_im, s5_log_step, s5_b_re, s5_b_im, s5_c_re, s5_c_im, s5_d,
            w_glu, g_ffn_o, w_router, w_exp_gate_up, w_exp_down, g_final):
    w_in = w_in_e[0].astype(BF16)
    w_out = w_out_e[0].astype(BF16)
    w_gu = w_gate_up[0].astype(BF16)
    w_dn = w_down[0].astype(BF16)
    w_gl = w_glu[0].astype(BF16)
    w_egu = w_exp_gate_up[0].astype(BF16)
    w_edn = w_exp_down[0].astype(BF16)

    shapes = [(x.shape[0], x.shape[1]) for x in xs]
    x1s, us = [], []
    for x in xs:
        b, s, d = x.shape
        z = _in_proj(x.reshape(b * s, d), g_mix_e[0], w_in, s).reshape(b, s, IN_COLS)
        a = _attention(z, lambda_qk[0], subln_g[0])
        x1 = _out_proj(a, z, conv_w[0], x, w_out).reshape(b * s, d)
        x1 = _ffn(x1, g_ffn_e[0], w_gu, w_dn)
        x1s.append(x1)
        us.append(_norm_bf16(x1, g_mix_o[0]))
    ys = _s5(us, shapes, s5_a_re[0], s5_a_im[0], s5_log_step[0], s5_b_re[0], s5_b_im[0], s5_c_re[0], s5_c_im[0])
    outs = []
    for x1, y, (b, s) in zip(x1s, ys, shapes):
        x2 = _glu(x1, y, g_mix_o[0], s5_d[0], w_gl)
        out = _moe(x2, g_ffn_o[0], w_router[0], w_egu, w_edn, g_final)
        outs.append(out.reshape(b, s, D_MODEL))
    return outs


def kernel(x_prompt, x_sample, g_mix_e, w_in_e, lambda_qk, subln_g, conv_w, w_out_e, g_ffn_e, w_gate_up, w_down, g_mix_o, s5_a_re, s5_a_im, s5_log_step, s5_b_re, s5_b_im, s5_c_re, s5_c_im, s5_d, w_glu, g_ffn_o, w_router, w_exp_gate_up, w_exp_down, g_final):
    y_prompt, y_sample = _trunks(
        [x_prompt, x_sample], g_mix_e, w_in_e, lambda_qk, subln_g, conv_w, w_out_e, g_ffn_e, w_gate_up, w_down,
        g_mix_o, s5_a_re, s5_a_im, s5_log_step, s5_b_re, s5_b_im, s5_c_re, s5_c_im, s5_d,
        w_glu, g_ffn_o, w_router, w_exp_gate_up, w_exp_down, g_final)
    return (y_prompt, y_sample)
```

```python
import functools
import math

import jax
import jax.numpy as jnp
from jax import lax
from jax.experimental import pallas as pl
from jax.experimental.pallas import tpu as pltpu

F32 = jnp.float32
BF16 = jnp.bfloat16

D_MODEL = 2048
ATTN_WIDTH = D_MODEL // 2
CONV_CH = D_MODEL - ATTN_WIDTH
N_HEADS = 8
QK_DIM = 64
V_DIM = 2 * QK_DIM
ROPE_DIM = QK_DIM // 4
ROPE_THETA = 500000.0
IN_COLS = 3 * ATTN_WIDTH + 3 * CONV_CH
S5_P = 16
S5_GROUPS = D_MODEL // S5_P
S5_N = 64
FF_DENSE = 5632
N_EXPERTS = 8
FF_EXPERT = 7168
RMS_EPS = 1e-5
LAMBDA_INIT_0 = 0.8 - 0.6 * math.exp(-0.3 * 0)

LANES = 128
SUBLANES = 8
V_PAD = V_DIM + 16
S5_CHUNK = 64
S5_ROW = S5_CHUNK * S5_P
VMEM_LIMIT = 52 * 1024 * 1024


def _params(sem):
    return pltpu.CompilerParams(dimension_semantics=sem, vmem_limit_bytes=VMEM_LIMIT)


def _rms(x, g):
    ms = jnp.mean(x * x, axis=-1, keepdims=True)
    return x * lax.rsqrt(ms + RMS_EPS) * g


def _inproj_kernel(x_ref, g_ref, w_ref, ra_ref, rm_ref, rp_ref, o_ref, h_ref, *, tn, n_q, n_rope):
    j = pl.program_id(1)

    @pl.when(j == 0)
    def _():
        h_ref[...] = _rms(x_ref[...], g_ref[...]).astype(BF16)

    acc = jnp.dot(h_ref[...], w_ref[...], preferred_element_type=F32)

    @pl.when(j < n_rope)
    def _():
        scale = jnp.where(j < n_q, QK_DIM ** -0.5, 1.0).astype(F32)
        a = ra_ref[...] * scale
        bm = rm_ref[...] * scale
        bp = rp_ref[...] * scale
        for c in range(tn // LANES):
            t = acc[:, c * LANES:(c + 1) * LANES]
            r = t * a + pltpu.roll(t, LANES - ROPE_DIM // 2, 1) * bm + pltpu.roll(t, ROPE_DIM // 2, 1) * bp
            o_ref[:, c * LANES:(c + 1) * LANES] = r.astype(o_ref.dtype)

    @pl.when(j >= n_rope)
    def _():
        o_ref[...] = acc.astype(o_ref.dtype)


def _rope_tables(seq):
    half = ROPE_DIM // 2
    inv = 1.0 / (ROPE_THETA ** (jnp.arange(0, ROPE_DIM, 2, dtype=F32) / ROPE_DIM))
    ang = jnp.arange(seq, dtype=F32)[:, None] * inv[None, :]
    cos, sin = jnp.cos(ang), jnp.sin(ang)
    ones = jnp.ones((seq, QK_DIM - ROPE_DIM), F32)
    zeros = jnp.zeros((seq, QK_DIM - ROPE_DIM), F32)
    zh = jnp.zeros((seq, half), F32)
    a = jnp.concatenate([cos, cos, ones], axis=-1)
    bm = jnp.concatenate([-sin, zh, zeros], axis=-1)
    bp = jnp.concatenate([zh, sin, zeros], axis=-1)
    rep = LANES // QK_DIM
    return jnp.tile(a, (1, rep)), jnp.tile(bm, (1, rep)), jnp.tile(bp, (1, rep))


def _in_proj(x2, g, w_bf, seq):
    t, d = x2.shape
    tm = min(1024, seq)
    tn = 1024
    ra, rm, rp = _rope_tables(seq)
    nblk = seq // tm
    rope_spec = pl.BlockSpec((tm, LANES), lambda i, j: (i % nblk, 0))
    return pl.pallas_call(
        functools.partial(_inproj_kernel, tn=tn, n_q=ATTN_WIDTH // tn, n_rope=2 * ATTN_WIDTH // tn),
        grid=(t // tm, IN_COLS // tn),
        in_specs=[
            pl.BlockSpec((tm, d), lambda i, j: (i, 0)),
            pl.BlockSpec((1, d), lambda i, j: (0, 0)),
            pl.BlockSpec((d, tn), lambda i, j: (0, j)),
            rope_spec, rope_spec, rope_spec,
        ],
        out_specs=pl.BlockSpec((tm, tn), lambda i, j: (i, j)),
        out_shape=jax.ShapeDtypeStruct((t, IN_COLS), BF16),
        scratch_shapes=[pltpu.VMEM((tm, d), BF16)],
        compiler_params=_params(("parallel", "arbitrary")),
        name="in_proj",
    )(x2, g.reshape(1, d), w_bf, ra, rm, rp)


def _attn_kernel(qt_ref, k_ref, vt_ref, lq_ref, sg_ref, o_ref, qq_ref, s0_ref, s1_ref, acc_ref,
                 *, tq, tk, nk):
    nq = 2 * tq
    qt = qt_ref[...]
    dim = lax.broadcasted_iota(jnp.int32, (V_DIM, tq), 0)
    zero = jnp.zeros_like(qt)
    qq_ref[:, 0:tq] = jnp.where(dim < QK_DIM, qt, zero)
    qq_ref[:, tq:nq] = jnp.where(dim >= QK_DIM, qt, zero)
    acc_ref[...] = jnp.zeros(acc_ref.shape, F32)

    def scores(j, s_ref):
        start = pl.multiple_of(j * tk, tk)
        s = jnp.dot(k_ref[pl.ds(start, tk), :], qq_ref[...], preferred_element_type=F32)
        s_ref[...] = s
        return jnp.max(s, axis=0, keepdims=True)

    def accumulate(j, s_ref, m_prev, block_max):
        m_new = jnp.maximum(m_prev, block_max)
        alpha = jnp.exp(m_prev - m_new)
        p = jnp.exp(s_ref[...] - m_new).astype(BF16)
        start = pl.multiple_of(j * tk, tk)
        pv = jnp.dot(vt_ref[:, pl.ds(start, tk)], p, preferred_element_type=F32)
        acc_ref[...] = alpha * acc_ref[...] + pv
        return m_new

    def body(jj, carry):
        m, max_even = carry
        j = 2 * jj
        max_odd = scores(j + 1, s1_ref)
        m = accumulate(j, s0_ref, m, max_even)
        max_even = scores(j + 2, s0_ref)
        m = accumulate(j + 1, s1_ref, m, max_odd)
        return m, max_even

    m0 = jnp.full((1, nq), -jnp.inf, F32)
    m, max_even = lax.fori_loop(0, nk // 2 - 1, body, (m0, scores(0, s0_ref)))
    max_odd = scores(nk - 1, s1_ref)
    m = accumulate(nk - 2, s0_ref, m, max_even)
    accumulate(nk - 1, s1_ref, m, max_odd)

    lq = lq_ref[...]
    lam = (jnp.exp(jnp.sum(lq[0:1, :] * lq[1:2, :], axis=-1, keepdims=True))
           - jnp.exp(jnp.sum(lq[2:3, :] * lq[3:4, :], axis=-1, keepdims=True)) + LAMBDA_INIT_0)
    acc = acc_ref[...]
    ot = acc[0:V_DIM, :] / acc[V_DIM:V_DIM + 1, :]
    diff = ot[:, 0:tq] - lam * ot[:, tq:nq]
    ms = jnp.mean(diff * diff, axis=0, keepdims=True)
    y = diff * lax.rsqrt(ms + RMS_EPS) * sg_ref[...] * (1.0 - LAMBDA_INIT_0)
    o_ref[...] = y.T.astype(o_ref.dtype)


def _attention(z3, lambda_qk, subln_g):
    b, s, _ = z3.shape
    tq = min(512, s)
    tk = min(512, s // 2)
    assert s % tq == 0 and s % (2 * tk) == 0, "key chunks are processed in pairs"
    kcol = ATTN_WIDTH // V_DIM
    heads = lambda t: t.reshape(b, s, N_HEADS, V_DIM).transpose(0, 2, 3, 1)
    qt = heads(z3[:, :, 0:ATTN_WIDTH])
    pad = jnp.concatenate([jnp.ones((b, N_HEADS, 1, s), BF16),
                           jnp.zeros((b, N_HEADS, V_PAD - V_DIM - 1, s), BF16)], axis=2)
    vt = jnp.concatenate([heads(z3[:, :, 2 * ATTN_WIDTH:3 * ATTN_WIDTH]), pad], axis=2)
    return pl.pallas_call(
        functools.partial(_attn_kernel, tq=tq, tk=tk, nk=s // tk),
        grid=(b, N_HEADS, s // tq),
        in_specs=[
            pl.BlockSpec((None, None, V_DIM, tq), lambda bi, h, i: (bi, h, 0, i)),
            pl.BlockSpec((None, s, V_DIM), lambda bi, h, i: (bi, 0, kcol + h)),
            pl.BlockSpec((None, None, V_PAD, s), lambda bi, h, i: (bi, h, 0, 0)),
            pl.BlockSpec((4, QK_DIM), lambda bi, h, i: (0, 0)),
            pl.BlockSpec((V_DIM, 1), lambda bi, h, i: (0, 0)),
        ],
        out_specs=pl.BlockSpec((None, tq, V_DIM), lambda bi, h, i: (bi, i, h)),
        out_shape=jax.ShapeDtypeStruct((b, s, ATTN_WIDTH), BF16),
        scratch_shapes=[
            pltpu.VMEM((V_DIM, 2 * tq), BF16),
            pltpu.VMEM((tk, 2 * tq), F32),
            pltpu.VMEM((tk, 2 * tq), F32),
            pltpu.VMEM((V_PAD, 2 * tq), F32),
        ],
        compiler_params=_params(("parallel", "parallel", "arbitrary")),
        name="diff_attention",
    )(qt, z3, vt, lambda_qk.astype(F32), subln_g.reshape(V_DIM, 1).astype(F32))


def _outproj_kernel(a_ref, bg_ref, cg_ref, xc_ref, cgp_ref, xcp_ref, cgn_ref, xcn_ref, cw_ref, x_ref, w_ref,
                    o_ref, *, tm, nt):
    i = pl.program_id(1)
    u = cg_ref[...].astype(F32) * xc_ref[...].astype(F32)
    u_before = cgp_ref[SUBLANES - 1:SUBLANES, :].astype(F32) * xcp_ref[SUBLANES - 1:SUBLANES, :].astype(F32)
    u_after = cgn_ref[0:1, :].astype(F32) * xcn_ref[0:1, :].astype(F32)
    u_before = u_before * (i > 0).astype(F32)
    u_after = u_after * (i < nt - 1).astype(F32)
    row = lax.broadcasted_iota(jnp.int32, u.shape, 0)
    u_prev = jnp.where(row == 0, u_before, pltpu.roll(u, 1, 0))
    u_next = jnp.where(row == tm - 1, u_after, pltpu.roll(u, tm - 1, 0))
    cw = cw_ref[...]
    y = u_prev * cw[0:1, :] + u * cw[1:2, :] + u_next * cw[2:3, :]
    c = (bg_ref[...].astype(F32) * y).astype(BF16)
    acc = jnp.dot(a_ref[...], w_ref[0:ATTN_WIDTH, :], preferred_element_type=F32)
    acc = acc + jnp.dot(c, w_ref[ATTN_WIDTH:D_MODEL, :], preferred_element_type=F32)
    o_ref[...] = x_ref[...] + acc


def _out_proj(a3, z3, conv_w, x3, w_bf):
    b, s, _ = z3.shape
    tm = min(512, s)
    nt = s // tm
    rb = tm // SUBLANES
    last = s // SUBLANES - 1
    cb = 3 * ATTN_WIDTH // CONV_CH
    main = lambda col: pl.BlockSpec((None, tm, CONV_CH), lambda bi, i: (bi, i, col))
    prev = lambda col: pl.BlockSpec((None, SUBLANES, CONV_CH), lambda bi, i: (bi, jnp.maximum(i * rb - 1, 0), col))
    nxt = lambda col: pl.BlockSpec((None, SUBLANES, CONV_CH), lambda bi, i: (bi, jnp.minimum((i + 1) * rb, last), col))
    return pl.pallas_call(
        functools.partial(_outproj_kernel, tm=tm, nt=nt),
        grid=(b, nt),
        in_specs=[
            pl.BlockSpec((None, tm, ATTN_WIDTH), lambda bi, i: (bi, i, 0)),
            main(cb), main(cb + 1), main(cb + 2),
            prev(cb + 1), prev(cb + 2), nxt(cb + 1), nxt(cb + 2),
            pl.BlockSpec((3, CONV_CH), lambda bi, i: (0, 0)),
            pl.BlockSpec((None, tm, D_MODEL), lambda bi, i: (bi, i, 0)),
            pl.BlockSpec((D_MODEL, D_MODEL), lambda bi, i: (0, 0)),
        ],
        out_specs=pl.BlockSpec((None, tm, D_MODEL), lambda bi, i: (bi, i, 0)),
        out_shape=jax.ShapeDtypeStruct((b, s, D_MODEL), F32),
        compiler_params=_params(("parallel", "arbitrary")),
        name="out_proj_conv",
    )(a3, z3, z3, z3, z3, z3, z3, z3, conv_w.astype(F32), x3, w_bf)


def _ffn_kernel(x_ref, g_ref, wg_ref, wu_ref, wd_ref, o_ref, h_ref):
    f = pl.program_id(1)

    @pl.when(f == 0)
    def _():
        x = x_ref[...]
        h_ref[...] = _rms(x, g_ref[...]).astype(BF16)
        o_ref[...] = x

    h = h_ref[...]
    gate = jnp.dot(h, wg_ref[...], preferred_element_type=F32)
    up = jnp.dot(h, wu_ref[...], preferred_element_type=F32)
    act = (gate * jax.nn.sigmoid(gate) * up).astype(BF16)
    o_ref[...] += jnp.dot(act, wd_ref[...], preferred_element_type=F32)


def _ffn(x2, g, wgu_bf, wd_bf):
    t, d = x2.shape
    tm = min(512, t)
    tf = 512
    nf = FF_DENSE // tf
    return pl.pallas_call(
        _ffn_kernel,
        grid=(t // tm, nf),
        in_specs=[
            pl.BlockSpec((tm, d), lambda i, f: (i, 0)),
            pl.BlockSpec((1, d), lambda i, f: (0, 0)),
            pl.BlockSpec((d, tf), lambda i, f: (0, f)),
            pl.BlockSpec((d, tf), lambda i, f: (0, f + nf)),
            pl.BlockSpec((tf, d), lambda i, f: (f, 0)),
        ],
        out_specs=pl.BlockSpec((tm, d), lambda i, f: (i, 0)),
        out_shape=jax.ShapeDtypeStruct((t, d), F32),
        scratch_shapes=[pltpu.VMEM((tm, d), BF16)],
        compiler_params=_params(("parallel", "arbitrary")),
        name="ffn_swiglu",
    )(x2, g.reshape(1, d), wgu_bf, wgu_bf, wd_bf)


def _norm_kernel(x_ref, g_ref, o_ref):
    o_ref[...] = _rms(x_ref[...], g_ref[...]).astype(o_ref.dtype)


def _norm_bf16(x2, g):
    t, d = x2.shape
    tm = min(1024, t)
    return pl.pallas_call(
        _norm_kernel,
        grid=(t // tm,),
        in_specs=[pl.BlockSpec((tm, d), lambda i: (i, 0)), pl.BlockSpec((1, d), lambda i: (0, 0))],
        out_specs=pl.BlockSpec((tm, d), lambda i: (i, 0)),
        out_shape=jax.ShapeDtypeStruct((t, d), BF16),
        compiler_params=_params(("parallel",)),
        name="s5_norm",
    )(x2, g.reshape(1, d))


_K_BP, _K_CP, _K_BW, _K_CS = 0, 1, 2, 3
_N_KIND = 4
_SCAN_STEPS_MAX = 16


def _s5_kernel(u_ref, pt_ref, yt_ref, al_ref, pos_ref, rem_ref, o_ref, tab_ref, *, rows, n_steps):
    def build(s, carry):
        for d in range(2):
            for kind in range(_N_KIND):
                x = pt_ref[d * 2 * _N_KIND + 2 * kind, pl.ds(s, 1), :]
                xs = pt_ref[d * 2 * _N_KIND + 2 * kind + 1, pl.ds(s, 1), :]
                par = 0 if kind in (_K_BP, _K_BW) else 2
                ya = yt_ref[d * 4 + par]
                yb = yt_ref[d * 4 + par + 1]
                val = x * ya + xs * yb
                tab_ref[d * _N_KIND + kind, pl.ds(pl.multiple_of(s * S5_P, S5_P), S5_P), :] = val.astype(BF16)
        return carry

    lax.fori_loop(0, S5_CHUNK, build, 0)

    nt = (((1,), (1,)), ((), ()))
    ri = lax.broadcasted_iota(jnp.int32, (S5_ROW, S5_ROW), 0) // S5_P
    ci = lax.broadcasted_iota(jnp.int32, (S5_ROW, S5_ROW), 1) // S5_P
    t_f = lax.dot_general(tab_ref[_K_BP], tab_ref[_K_CP], nt, preferred_element_type=F32)
    t_b = lax.dot_general(tab_ref[_N_KIND + _K_BP], tab_ref[_N_KIND + _K_CP], nt, preferred_element_type=F32)
    toep = (jnp.where(ri <= ci, t_f, 0.0) + jnp.where(ri >= ci, t_b, 0.0)).astype(BF16)

    u = u_ref[...]
    y = jnp.dot(u, toep, preferred_element_type=F32)

    lane = lax.broadcasted_iota(jnp.int32, (rows, LANES), 1)
    for d in range(2):
        fwd = d == 0
        dist = pos_ref[...] if fwd else rem_ref[...]
        st = jnp.dot(u, tab_ref[d * _N_KIND + _K_BW], preferred_element_type=F32)
        for k in range(n_steps):
            step = 1 << k
            sh = pltpu.roll(st, step if fwd else rows - step, 0)
            sh = jnp.where(dist >= step, sh, 0.0)
            shs = pltpu.roll(sh, S5_N, 1)
            pa = al_ref[d * 2 * _SCAN_STEPS_MAX + k:d * 2 * _SCAN_STEPS_MAX + k + 1, :]
            pb = al_ref[d * 2 * _SCAN_STEPS_MAX + _SCAN_STEPS_MAX + k:d * 2 * _SCAN_STEPS_MAX + _SCAN_STEPS_MAX + k + 1, :]
            st = st + sh * pa + shs * pb
        inc = pltpu.roll(st, 1 if fwd else rows - 1, 0)
        inc = jnp.where(dist >= 1, inc, 0.0).astype(BF16)
        y = y + lax.dot_general(inc, tab_ref[d * _N_KIND + _K_CS], nt, preferred_element_type=F32)
    del lane
    o_ref[...] = y.astype(o_ref.dtype)


def _s5_tables(a_re, a_im, log_step, b_re, b_im, c_re, c_im, n_steps):
    dt = jnp.exp(log_step.astype(F32))[..., None]
    a_re = a_re.astype(F32)
    a_im = a_im.astype(F32)
    lr, li = a_re * dt, a_im * dt
    er = jnp.exp(lr)
    abr, abi = er * jnp.cos(li), er * jnp.sin(li)
    den = a_re * a_re + a_im * a_im
    fr = ((abr - 1.0) * a_re + abi * a_im) / den
    fi = (abi * a_re - (abr - 1.0) * a_im) / den
    b_re = b_re.astype(F32)
    b_im = b_im.astype(F32)
    bbr = fr[..., None] * b_re - fi[..., None] * b_im
    bbi = fr[..., None] * b_im + fi[..., None] * b_re
    bbr, bbi = jnp.swapaxes(bbr, -1, -2), jnp.swapaxes(bbi, -1, -2)
    cr, ci = c_re.astype(F32), c_im.astype(F32)
    cat = lambda lo, hi: jnp.concatenate([lo, hi], axis=-1)
    yt = jnp.stack([cat(bbr, bbr), cat(-bbi, bbi), cat(cr, -cr), cat(-ci, -ci)], axis=2)
    yt = jnp.swapaxes(yt, 0, 1).reshape(S5_GROUPS, 2 * 4, S5_P, LANES)

    idx = jnp.arange(S5_CHUNK, dtype=F32)
    length = float(S5_CHUNK)
    expo = jnp.stack([
        jnp.stack([-idx, idx, length - 1.0 - idx, idx + 1.0]),
        jnp.stack([idx, -idx, idx, length - idx]),
    ])
    e = expo[:, None, :, :, None]
    mag = jnp.exp(lr[:, :, None, None, :] * e)
    ang = li[:, :, None, None, :] * e
    pr, pi = mag * jnp.cos(ang), mag * jnp.sin(ang)
    pt = jnp.stack([cat(pr, pi), cat(pi, pr)], axis=3)
    pt = jnp.swapaxes(pt, 0, 1).reshape(S5_GROUPS, 2 * _N_KIND * 2, S5_CHUNK, LANES)

    steps = length * (2.0 ** jnp.arange(_SCAN_STEPS_MAX, dtype=F32))
    se = steps[None, None, :, None]
    smag = jnp.exp(lr[:, :, None, :] * se)
    sang = li[:, :, None, :] * se
    sr, si = smag * jnp.cos(sang), smag * jnp.sin(sang)
    al = jnp.concatenate([cat(sr, sr), cat(-si, si)], axis=2)
    al = jnp.swapaxes(al, 0, 1).reshape(S5_GROUPS, 2 * 2 * _SCAN_STEPS_MAX, LANES)
    del n_steps
    return pt, yt, al


def _s5(u_list, shapes, a_re, a_im, log_step, b_re, b_im, c_re, c_im):
    pos_parts, rem_parts, max_cps = [], [], 1
    for (b, s) in shapes:
        cps = s // S5_CHUNK
        max_cps = max(max_cps, cps)
        p = jnp.tile(jnp.arange(cps, dtype=jnp.int32), b)
        pos_parts.append(p)
        rem_parts.append(cps - 1 - p)
    n_steps = max(1, (max_cps - 1).bit_length())
    pos = jnp.concatenate(pos_parts)
    rem = jnp.concatenate(rem_parts)
    rows = pos.shape[0]
    pos = jnp.broadcast_to(pos[:, None], (rows, LANES))
    rem = jnp.broadcast_to(rem[:, None], (rows, LANES))
    pt, yt, al = _s5_tables(a_re, a_im, log_step, b_re, b_im, c_re, c_im, n_steps)

    u_all = jnp.concatenate(u_list, axis=0)
    u_t = u_all.reshape(rows, S5_CHUNK, S5_GROUPS, S5_P).transpose(2, 0, 1, 3).reshape(S5_GROUPS, rows, S5_ROW)
    y_t = pl.pallas_call(
        functools.partial(_s5_kernel, rows=rows, n_steps=n_steps),
        grid=(S5_GROUPS,),
        in_specs=[
            pl.BlockSpec((None, rows, S5_ROW), lambda g: (g, 0, 0)),
            pl.BlockSpec((None, 2 * _N_KIND * 2, S5_CHUNK, LANES), lambda g: (g, 0, 0, 0)),
            pl.BlockSpec((None, 2 * 4, S5_P, LANES), lambda g: (g, 0, 0, 0)),
            pl.BlockSpec((None, 2 * 2 * _SCAN_STEPS_MAX, LANES), lambda g: (g, 0, 0)),
            pl.BlockSpec((rows, LANES), lambda g: (0, 0)),
            pl.BlockSpec((rows, LANES), lambda g: (0, 0)),
        ],
        out_specs=pl.BlockSpec((None, rows, S5_ROW), lambda g: (g, 0, 0)),
        out_shape=jax.ShapeDtypeStruct((S5_GROUPS, rows, S5_ROW), BF16),
        scratch_shapes=[pltpu.VMEM((2 * _N_KIND, S5_ROW, LANES), BF16)],
        compiler_params=_params(("parallel",)),
        name="s5_chunked",
    )(u_t, pt, yt, al, pos, rem)
    y_all = y_t.reshape(S5_GROUPS, rows, S5_CHUNK, S5_P).transpose(1, 2, 0, 3).reshape(rows * S5_CHUNK, D_MODEL)
    outs, off = [], 0
    for (b, s) in shapes:
        outs.append(y_all[off:off + b * s])
        off += b * s
    return outs


def _glu_kernel(x_ref, y_ref, g_ref, d_ref, wa_ref, wb_ref, o_ref, a_ref, *, tn):
    j = pl.program_id(1)

    @pl.when(j == 0)
    def _():
        h = _rms(x_ref[...], g_ref[...])
        t = (y_ref[...].astype(F32) + d_ref[...] * h)
        gelu = 0.5 * t * (1.0 + jnp.tanh(math.sqrt(2.0 / math.pi) * (t + 0.044715 * (t * t * t))))
        a_ref[...] = gelu.astype(BF16)

    a = a_ref[...]
    ga = jnp.dot(a, wa_ref[...], preferred_element_type=F32)
    gb = jnp.dot(a, wb_ref[...], preferred_element_type=F32)
    xs = x_ref[:, pl.ds(pl.multiple_of(j * tn, tn), tn)]
    o_ref[...] = xs + ga * jax.nn.sigmoid(gb)


def _glu(x2, y2, g, dskip, w_bf):
    t, d = x2.shape
    tm = min(512, t)
    tn = 1024
    nj = d // tn
    return pl.pallas_call(
        functools.partial(_glu_kernel, tn=tn),
        grid=(t // tm, nj),
        in_specs=[
            pl.BlockSpec((tm, d), lambda i, j: (i, 0)),
            pl.BlockSpec((tm, d), lambda i, j: (i, 0)),
            pl.BlockSpec((1, d), lambda i, j: (0, 0)),
            pl.BlockSpec((1, d), lambda i, j: (0, 0)),
            pl.BlockSpec((d, tn), lambda i, j: (0, j)),
            pl.BlockSpec((d, tn), lambda i, j: (0, j + nj)),
        ],
        out_specs=pl.BlockSpec((tm, tn), lambda i, j: (i, j)),
        out_shape=jax.ShapeDtypeStruct((t, d), F32),
        scratch_shapes=[pltpu.VMEM((tm, d), BF16)],
        compiler_params=_params(("parallel", "arbitrary")),
        name="s5_glu",
    )(x2, y2, g.reshape(1, d), dskip.reshape(1, d).astype(F32), w_bf, w_bf)


MOE_TILE = 512
MOE_FF_TILE = 512
COMBINE_TILE = 256


def _route_kernel(x_ref, g_ref, wr_ref, idx_ref, gate_ref):
    h = _rms(x_ref[...], g_ref[...])
    logits = jnp.dot(h, wr_ref[...], preferred_element_type=F32, precision=lax.Precision.HIGHEST)
    lane = lax.broadcasted_iota(jnp.int32, logits.shape, 1)
    neg = jnp.float32(-jnp.inf)
    lg = jnp.where(lane < N_EXPERTS, logits, neg)
    m1 = jnp.max(lg, axis=-1, keepdims=True)
    i1 = jnp.min(jnp.where(lg == m1, lane, LANES), axis=-1, keepdims=True)
    lg2 = jnp.where(lane == i1, neg, lg)
    m2 = jnp.max(lg2, axis=-1, keepdims=True)
    i2 = jnp.min(jnp.where(lg2 == m2, lane, LANES), axis=-1, keepdims=True)
    ex = jnp.exp(m2 - m1)
    den = 1.0 + ex
    idx_ref[...] = jnp.where(lane == 0, i1, jnp.where(lane == 1, i2, 0))
    gate_ref[...] = jnp.where(lane == 0, 1.0 / den, jnp.where(lane == 1, ex / den, 0.0))


def _route(x2, g, w_router):
    t, d = x2.shape
    tm = min(512, t)
    wr = jnp.zeros((d, LANES), F32).at[:, :N_EXPERTS].set(w_router.astype(F32))
    return pl.pallas_call(
        _route_kernel,
        grid=(t // tm,),
        in_specs=[
            pl.BlockSpec((tm, d), lambda i: (i, 0)),
            pl.BlockSpec((1, d), lambda i: (0, 0)),
            pl.BlockSpec((d, LANES), lambda i: (0, 0)),
        ],
        out_specs=[pl.BlockSpec((tm, LANES), lambda i: (i, 0)), pl.BlockSpec((tm, LANES), lambda i: (i, 0))],
        out_shape=[jax.ShapeDtypeStruct((t, LANES), jnp.int32), jax.ShapeDtypeStruct((t, LANES), F32)],
        compiler_params=_params(("parallel",)),
        name="moe_route",
    )(x2, g.reshape(1, d), wr)


def _expert_kernel(src_ref, te_ref, tv_ref, x_hbm, g_ref, wg_ref, wu_ref, wd_ref, o_ref, xbuf, hb_ref, sem,
                   *, tm, n_tiles):
    i = pl.program_id(0)
    f = pl.program_id(1)
    slot = i % 2

    def row_copy(tile, r, buf):
        tok = src_ref[tile * tm + r]
        return pltpu.make_async_copy(x_hbm.at[pl.ds(tok, 1)], xbuf.at[buf, pl.ds(r, 1)], sem.at[buf])

    def start_gather(tile, buf):
        def go(r, c):
            row_copy(tile, r, buf).start()
            return c
        lax.fori_loop(0, tm, go, 0)

    def wait_gather(tile, buf):
        def go(r, c):
            row_copy(tile, r, buf).wait()
            return c
        lax.fori_loop(0, tm, go, 0)

    @pl.when(f == 0)
    def _():
        @pl.when(i == 0)
        def _():
            start_gather(0, 0)

        @pl.when(i + 1 < n_tiles)
        def _():
            start_gather(i + 1, 1 - slot)

        wait_gather(i, slot)
        hb_ref[...] = _rms(xbuf[slot], g_ref[...]).astype(BF16)
        o_ref[...] = jnp.zeros(o_ref.shape, F32)

    @pl.when(tv_ref[i] != 0)
    def _():
        h = hb_ref[...]
        gate = jnp.dot(h, wg_ref[...], preferred_element_type=F32)
        up = jnp.dot(h, wu_ref[...], preferred_element_type=F32)
        act = (gate * jax.nn.sigmoid(gate) * up).astype(BF16)
        o_ref[...] += jnp.dot(act, wd_ref[...], preferred_element_type=F32)


def _combine_kernel(dest_ref, x_ref, gate_ref, gf_ref, y_hbm, o_ref, ybuf, sem, *, tm, n_tiles):
    i = pl.program_id(0)
    slot = i % 2

    def row_copy(tile, r, k, buf):
        row = dest_ref[(tile * tm + r) * 2 + k]
        return pltpu.make_async_copy(y_hbm.at[pl.ds(row, 1)], ybuf.at[buf, k, pl.ds(r, 1)], sem.at[buf])

    def start_gather(tile, buf):
        def go(r, c):
            row_copy(tile, r, 0, buf).start()
            row_copy(tile, r, 1, buf).start()
            return c
        lax.fori_loop(0, tm, go, 0)

    def wait_gather(tile, buf):
        def go(r, c):
            row_copy(tile, r, 0, buf).wait()
            row_copy(tile, r, 1, buf).wait()
            return c
        lax.fori_loop(0, tm, go, 0)

    @pl.when(i == 0)
    def _():
        start_gather(0, 0)

    @pl.when(i + 1 < n_tiles)
    def _():
        start_gather(i + 1, 1 - slot)

    wait_gather(i, slot)
    gate = gate_ref[...]
    y = x_ref[...] + gate[:, 0:1] * ybuf[slot, 0] + gate[:, 1:2] * ybuf[slot, 1]
    o_ref[...] = _rms(y, gf_ref[...])


def _moe(x2, g, w_router, wgu_bf, wd_bf, g_final):
    t, d = x2.shape
    tm = MOE_TILE if t >= 8 * MOE_TILE else 64
    tf = MOE_FF_TILE
    nf = FF_EXPERT // tf
    ridx, rgate = _route(x2, g, w_router)

    n_assign = 2 * t
    expert = ridx[:, 0:2].reshape(n_assign)
    onehot = (expert[:, None] == jnp.arange(N_EXPERTS, dtype=jnp.int32)[None, :]).astype(jnp.int32)
    csum = jnp.cumsum(onehot, axis=0)
    rank = jnp.sum((csum - onehot) * onehot, axis=1)
    padded = ((csum[-1] + tm - 1) // tm) * tm
    pend = jnp.cumsum(padded)
    dest = jnp.sum(onehot * (pend - padded)[None, :], axis=1) + rank
    n_slots = n_assign + N_EXPERTS * tm
    n_tiles = n_slots // tm
    src = jnp.zeros((n_slots,), jnp.int32).at[dest].set(jnp.arange(n_assign, dtype=jnp.int32) // 2)
    tile_start = jnp.arange(n_tiles, dtype=jnp.int32) * tm
    tile_expert = jnp.minimum(jnp.sum((tile_start[:, None] >= pend[None, :]).astype(jnp.int32), axis=1),
                              N_EXPERTS - 1)
    tile_valid = (tile_start < pend[-1]).astype(jnp.int32)

    wcol = lambda i, f, tv: jnp.where(tv[i] != 0, f, nf - 1)
    ys = pl.pallas_call(
        functools.partial(_expert_kernel, tm=tm, n_tiles=n_tiles),
        grid_spec=pltpu.PrefetchScalarGridSpec(
            num_scalar_prefetch=3,
            grid=(n_tiles, nf),
            in_specs=[
                pl.BlockSpec(memory_space=pl.ANY),
                pl.BlockSpec((1, d), lambda i, f, src, te, tv: (0, 0)),
                pl.BlockSpec((None, d, tf), lambda i, f, src, te, tv: (te[i], 0, wcol(i, f, tv))),
                pl.BlockSpec((None, d, tf), lambda i, f, src, te, tv: (te[i], 0, wcol(i, f, tv) + nf)),
                pl.BlockSpec((None, tf, d), lambda i, f, src, te, tv: (te[i], wcol(i, f, tv), 0)),
            ],
            out_specs=pl.BlockSpec((tm, d), lambda i, f, src, te, tv: (i, 0)),
            scratch_shapes=[
                pltpu.VMEM((2, tm, d), F32),
                pltpu.VMEM((tm, d), BF16),
                pltpu.SemaphoreType.DMA((2,)),
            ],
        ),
        out_shape=jax.ShapeDtypeStruct((n_slots, d), F32),
        compiler_params=_params(("arbitrary", "arbitrary")),
        name="moe_experts",
    )(src, tile_expert, tile_valid, x2, g.reshape(1, d), wgu_bf, wgu_bf, wd_bf)

    tc = min(COMBINE_TILE, t)
    nc = t // tc
    return pl.pallas_call(
        functools.partial(_combine_kernel, tm=tc, n_tiles=nc),
        grid_spec=pltpu.PrefetchScalarGridSpec(
            num_scalar_prefetch=1,
            grid=(nc,),
            in_specs=[
                pl.BlockSpec((tc, d), lambda i, dst: (i, 0)),
                pl.BlockSpec((tc, LANES), lambda i, dst: (i, 0)),
                pl.BlockSpec((1, d), lambda i, dst: (0, 0)),
                pl.BlockSpec(memory_space=pl.ANY),
            ],
            out_specs=pl.BlockSpec((tc, d), lambda i, dst: (i, 0)),
            scratch_shapes=[
                pltpu.VMEM((2, 2, tc, d), F32),
                pltpu.SemaphoreType.DMA((2,)),
            ],
        ),
        out_shape=jax.ShapeDtypeStruct((t, d), F32),
        compiler_params=_params(("arbitrary",)),
        name="moe_combine",
    )(dest, x2, rgate, g_final.reshape(1, d), ys)


def _trunks(xs, g_mix_e, w_in_e, lambda_qk, subln_g, conv_w, w_out_e, g_ffn_e, w_gate_up, w_down,
            g_mix_o, s5_a_re, s5_a_im, s5_log_step, s5_b_re, s5_b_im, s5_c_re, s5_c_im, s5_d,
            w_glu, g_ffn_o, w_router, w_exp_gate_up, w_exp_down, g_final):
    w_in = w_in_e[0].astype(BF16)
    w_out = w_out_e[0].astype(BF16)
    w_gu = w_gate_up[0].astype(BF16)
    w_dn = w_down[0].astype(BF16)
    w_gl = w_glu[0].astype(BF16)
    w_egu = w_exp_gate_up[0].astype(BF16)
    w_edn = w_exp_down[0].astype(BF16)

    shapes = [(x.shape[0], x.shape[1]) for x in xs]
    x1s, us = [], []
    for x in xs:
        b, s, d = x.shape
        z = _in_proj(x.reshape(b * s, d), g_mix_e[0], w_in, s).reshape(b, s, IN_COLS)
        a = _attention(z, lambda_qk[0], subln_g[0])
        x1 = _out_proj(a, z, conv_w[0], x, w_out).reshape(b * s, d)
        x1 = _ffn(x1, g_ffn_e[0], w_gu, w_dn)
        x1s.append(x1)
        us.append(_norm_bf16(x1, g_mix_o[0]))
    ys = _s5(us, shapes, s5_a_re[0], s5_a_im[0], s5_log_step[0], s5_b_re[0], s5_b_im[0], s5_c_re[0], s5_c_im[0])
    outs = []
    for x1, y, (b, s) in zip(x1s, ys, shapes):
        x2 = _glu(x1, y, g_mix_o[0], s5_d[0], w_gl)
        out = _moe(x2, g_ffn_o[0], w_router[0], w_egu, w_edn, g_final)
        outs.append(out.reshape(b, s, D_MODEL))
    return outs


def kernel(x_prompt, x_sample, g_mix_e, w_in_e, lambda_qk, subln_g, conv_w, w_out_e, g_ffn_e, w_gate_up, w_down, g_mix_o, s5_a_re, s5_a_im, s5_log_step, s5_b_re, s5_b_im, s5_c_re, s5_c_im, s5_d, w_glu, g_ffn_o, w_router, w_exp_gate_up, w_exp_down, g_final):
    y_prompt, y_sample = _trunks(
        [x_prompt, x_sample], g_mix_e, w_in_e, lambda_qk, subln_g, conv_w, w_out_e, g_ffn_e, w_gate_up, w_down,
        g_mix_o, s5_a_re, s5_a_im, s5_log_step, s5_b_re, s5_b_im, s5_c_re, s5_c_im, s5_d,
        w_glu, g_ffn_o, w_router, w_exp_gate_up, w_exp_down, g_final)
    return (y_prompt, y_sample)
```

```python
import functools
import math

import jax
import jax.numpy as jnp
from jax import lax
from jax.experimental import pallas as pl
from jax.experimental.pallas import tpu as pltpu

F32 = jnp.float32
BF16 = jnp.bfloat16

D_MODEL = 2048
ATTN_WIDTH = D_MODEL // 2
CONV_CH = D_MODEL - ATTN_WIDTH
N_HEADS = 8
QK_DIM = 64
V_DIM = 2 * QK_DIM
ROPE_DIM = QK_DIM // 4
ROPE_THETA = 500000.0
IN_COLS = 3 * ATTN_WIDTH + 3 * CONV_CH
S5_P = 16
S5_GROUPS = D_MODEL // S5_P
S5_N = 64
FF_DENSE = 5632
N_EXPERTS = 8
FF_EXPERT = 7168
RMS_EPS = 1e-5
LAMBDA_INIT_0 = 0.8 - 0.6 * math.exp(-0.3 * 0)
QK_SCALE_LOG2 = QK_DIM ** -0.5 * math.log2(math.e)

LANES = 128
SUBLANES = 8
S5_CHUNK = 64
S5_ROW = S5_CHUNK * S5_P
VMEM_LIMIT = 52 * 1024 * 1024


def _params(sem):
    return pltpu.CompilerParams(dimension_semantics=sem, vmem_limit_bytes=VMEM_LIMIT)


def _rms(x, g):
    ms = jnp.mean(x * x, axis=-1, keepdims=True)
    return x * lax.rsqrt(ms + RMS_EPS) * g


def _inproj_kernel(x_ref, g_ref, w_ref, ra_ref, rm_ref, rp_ref, o_ref, h_ref, *, tn, n_q, n_rope):
    j = pl.program_id(1)

    @pl.when(j == 0)
    def _():
        h_ref[...] = _rms(x_ref[...], g_ref[...]).astype(BF16)

    acc = jnp.dot(h_ref[...], w_ref[...], preferred_element_type=F32)

    @pl.when(j < n_rope)
    def _():
        scale = jnp.where(j < n_q, QK_SCALE_LOG2, 1.0).astype(F32)
        a = ra_ref[...] * scale
        bm = rm_ref[...] * scale
        bp = rp_ref[...] * scale
        for c in range(tn // LANES):
            t = acc[:, c * LANES:(c + 1) * LANES]
            r = t * a + pltpu.roll(t, LANES - ROPE_DIM // 2, 1) * bm + pltpu.roll(t, ROPE_DIM // 2, 1) * bp
            o_ref[:, c * LANES:(c + 1) * LANES] = r.astype(o_ref.dtype)

    @pl.when(j >= n_rope)
    def _():
        o_ref[...] = acc.astype(o_ref.dtype)


def _rope_tables(seq):
    half = ROPE_DIM // 2
    inv = 1.0 / (ROPE_THETA ** (jnp.arange(0, ROPE_DIM, 2, dtype=F32) / ROPE_DIM))
    ang = jnp.arange(seq, dtype=F32)[:, None] * inv[None, :]
    cos, sin = jnp.cos(ang), jnp.sin(ang)
    ones = jnp.ones((seq, QK_DIM - ROPE_DIM), F32)
    zeros = jnp.zeros((seq, QK_DIM - ROPE_DIM), F32)
    zh = jnp.zeros((seq, half), F32)
    a = jnp.concatenate([cos, cos, ones], axis=-1)
    bm = jnp.concatenate([-sin, zh, zeros], axis=-1)
    bp = jnp.concatenate([zh, sin, zeros], axis=-1)
    rep = LANES // QK_DIM
    return jnp.tile(a, (1, rep)), jnp.tile(bm, (1, rep)), jnp.tile(bp, (1, rep))


def _in_proj(x2, g, w_bf, seq):
    t, d = x2.shape
    tm = min(1024, seq)
    tn = 1024
    ra, rm, rp = _rope_tables(seq)
    nblk = seq // tm
    rope_spec = pl.BlockSpec((tm, LANES), lambda i, j: (i % nblk, 0))
    return pl.pallas_call(
        functools.partial(_inproj_kernel, tn=tn, n_q=ATTN_WIDTH // tn, n_rope=2 * ATTN_WIDTH // tn),
        grid=(t // tm, IN_COLS // tn),
        in_specs=[
            pl.BlockSpec((tm, d), lambda i, j: (i, 0)),
            pl.BlockSpec((1, d), lambda i, j: (0, 0)),
            pl.BlockSpec((d, tn), lambda i, j: (0, j)),
            rope_spec, rope_spec, rope_spec,
        ],
        out_specs=pl.BlockSpec((tm, tn), lambda i, j: (i, j)),
        out_shape=jax.ShapeDtypeStruct((t, IN_COLS), BF16),
        scratch_shapes=[pltpu.VMEM((tm, d), BF16)],
        compiler_params=_params(("parallel", "arbitrary")),
        name="in_proj",
    )(x2, g.reshape(1, d), w_bf, ra, rm, rp)


def _attn_kernel(q_ref, k_ref, v_ref, lq_ref, sg_ref, o_ref, qq_ref, vt_ref, s0_ref, s1_ref, acc_ref,
                 *, tq, tk, nk):
    nq = 2 * tq

    @pl.when(pl.program_id(2) == 0)
    def _():
        def flip(c, carry):
            start = pl.multiple_of(c * tk, tk)
            vt_ref[:, pl.ds(start, tk)] = v_ref[pl.ds(start, tk), :].astype(F32).T.astype(BF16)
            return carry
        lax.fori_loop(0, nk, flip, 0)

    qt = q_ref[...].astype(F32).T.astype(BF16)
    dim = lax.broadcasted_iota(jnp.int32, (V_DIM, tq), 0)
    zero = jnp.zeros_like(qt)
    qq_ref[:, 0:tq] = jnp.where(dim < QK_DIM, qt, zero)
    qq_ref[:, tq:nq] = jnp.where(dim >= QK_DIM, qt, zero)
    acc_ref[...] = jnp.zeros(acc_ref.shape, F32)

    def scores(j, s_ref):
        start = pl.multiple_of(j * tk, tk)
        s = jnp.dot(k_ref[pl.ds(start, tk), :], qq_ref[...], preferred_element_type=F32)
        s_ref[...] = s
        return jnp.max(s, axis=0, keepdims=True)

    def accumulate(j, s_ref, carry, block_max):
        m_prev, l_prev = carry
        m_new = jnp.maximum(m_prev, block_max)
        alpha = jnp.exp2(m_prev - m_new)
        p = jnp.exp2(s_ref[...] - m_new)
        l_new = alpha * l_prev + jnp.sum(p, axis=0, keepdims=True)
        start = pl.multiple_of(j * tk, tk)
        pv = jnp.dot(vt_ref[:, pl.ds(start, tk)], p.astype(BF16), preferred_element_type=F32)
        acc_ref[...] = alpha * acc_ref[...] + pv
        return m_new, l_new

    def body(jj, carry):
        ml, max_even = carry
        j = 2 * jj
        max_odd = scores(j + 1, s1_ref)
        ml = accumulate(j, s0_ref, ml, max_even)
        max_even = scores(j + 2, s0_ref)
        ml = accumulate(j + 1, s1_ref, ml, max_odd)
        return ml, max_even

    ml0 = (jnp.full((1, nq), -jnp.inf, F32), jnp.zeros((1, nq), F32))
    ml, max_even = lax.fori_loop(0, nk // 2 - 1, body, (ml0, scores(0, s0_ref)))
    max_odd = scores(nk - 1, s1_ref)
    ml = accumulate(nk - 2, s0_ref, ml, max_even)
    _, denom = accumulate(nk - 1, s1_ref, ml, max_odd)

    lq = lq_ref[...]
    lam = (jnp.exp(jnp.sum(lq[0:1, :] * lq[1:2, :], axis=-1, keepdims=True))
           - jnp.exp(jnp.sum(lq[2:3, :] * lq[3:4, :], axis=-1, keepdims=True)) + LAMBDA_INIT_0)
    ot = acc_ref[...] / denom
    diff = ot[:, 0:tq] - lam * ot[:, tq:nq]
    ms = jnp.mean(diff * diff, axis=0, keepdims=True)
    y = diff * lax.rsqrt(ms + RMS_EPS) * sg_ref[...] * (1.0 - LAMBDA_INIT_0)
    o_ref[...] = y.T.astype(o_ref.dtype)


def _attention(z3, lambda_qk, subln_g):
    b, s, _ = z3.shape
    tq = min(512, s)
    tk = min(512, s // 2)
    assert s % tq == 0 and s % (2 * tk) == 0, "key chunks are processed in pairs"
    kcol = ATTN_WIDTH // V_DIM
    return pl.pallas_call(
        functools.partial(_attn_kernel, tq=tq, tk=tk, nk=s // tk),
        grid=(b, N_HEADS, s // tq),
        in_specs=[
            pl.BlockSpec((None, tq, V_DIM), lambda bi, h, i: (bi, i, h)),
            pl.BlockSpec((None, s, V_DIM), lambda bi, h, i: (bi, 0, kcol + h)),
            pl.BlockSpec((None, s, V_DIM), lambda bi, h, i: (bi, 0, 2 * kcol + h)),
            pl.BlockSpec((4, QK_DIM), lambda bi, h, i: (0, 0)),
            pl.BlockSpec((V_DIM, 1), lambda bi, h, i: (0, 0)),
        ],
        out_specs=pl.BlockSpec((None, tq, V_DIM), lambda bi, h, i: (bi, i, h)),
        out_shape=jax.ShapeDtypeStruct((b, s, ATTN_WIDTH), BF16),
        scratch_shapes=[
            pltpu.VMEM((V_DIM, 2 * tq), BF16),
            pltpu.VMEM((V_DIM, s), BF16),
            pltpu.VMEM((tk, 2 * tq), F32),
            pltpu.VMEM((tk, 2 * tq), F32),
            pltpu.VMEM((V_DIM, 2 * tq), F32),
        ],
        compiler_params=_params(("parallel", "parallel", "arbitrary")),
        name="diff_attention",
    )(z3, z3, z3, lambda_qk.astype(F32), subln_g.reshape(V_DIM, 1).astype(F32))


def _outproj_kernel(a_ref, bg_ref, cg_ref, xc_ref, cgp_ref, xcp_ref, cgn_ref, xcn_ref, cw_ref, x_ref, w_ref,
                    o_ref, *, tm, nt):
    i = pl.program_id(1)
    u = cg_ref[...].astype(F32) * xc_ref[...].astype(F32)
    u_before = cgp_ref[SUBLANES - 1:SUBLANES, :].astype(F32) * xcp_ref[SUBLANES - 1:SUBLANES, :].astype(F32)
    u_after = cgn_ref[0:1, :].astype(F32) * xcn_ref[0:1, :].astype(F32)
    u_before = u_before * (i > 0).astype(F32)
    u_after = u_after * (i < nt - 1).astype(F32)
    row = lax.broadcasted_iota(jnp.int32, u.shape, 0)
    u_prev = jnp.where(row == 0, u_before, pltpu.roll(u, 1, 0))
    u_next = jnp.where(row == tm - 1, u_after, pltpu.roll(u, tm - 1, 0))
    cw = cw_ref[...]
    y = u_prev * cw[0:1, :] + u * cw[1:2, :] + u_next * cw[2:3, :]
    c = (bg_ref[...].astype(F32) * y).astype(BF16)
    acc = jnp.dot(a_ref[...], w_ref[0:ATTN_WIDTH, :], preferred_element_type=F32)
    acc = acc + jnp.dot(c, w_ref[ATTN_WIDTH:D_MODEL, :], preferred_element_type=F32)
    o_ref[...] = x_ref[...] + acc


def _out_proj(a3, z3, conv_w, x3, w_bf):
    b, s, _ = z3.shape
    tm = min(512, s)
    nt = s // tm
    rb = tm // SUBLANES
    last = s // SUBLANES - 1
    cb = 3 * ATTN_WIDTH // CONV_CH
    main = lambda col: pl.BlockSpec((None, tm, CONV_CH), lambda bi, i: (bi, i, col))
    prev = lambda col: pl.BlockSpec((None, SUBLANES, CONV_CH), lambda bi, i: (bi, jnp.maximum(i * rb - 1, 0), col))
    nxt = lambda col: pl.BlockSpec((None, SUBLANES, CONV_CH), lambda bi, i: (bi, jnp.minimum((i + 1) * rb, last), col))
    return pl.pallas_call(
        functools.partial(_outproj_kernel, tm=tm, nt=nt),
        grid=(b, nt),
        in_specs=[
            pl.BlockSpec((None, tm, ATTN_WIDTH), lambda bi, i: (bi, i, 0)),
            main(cb), main(cb + 1), main(cb + 2),
            prev(cb + 1), prev(cb + 2), nxt(cb + 1), nxt(cb + 2),
            pl.BlockSpec((3, CONV_CH), lambda bi, i: (0, 0)),
            pl.BlockSpec((None, tm, D_MODEL), lambda bi, i: (bi, i, 0)),
            pl.BlockSpec((D_MODEL, D_MODEL), lambda bi, i: (0, 0)),
        ],
        out_specs=pl.BlockSpec((None, tm, D_MODEL), lambda bi, i: (bi, i, 0)),
        out_shape=jax.ShapeDtypeStruct((b, s, D_MODEL), F32),
        compiler_params=_params(("parallel", "arbitrary")),
        name="out_proj_conv",
    )(a3, z3, z3, z3, z3, z3, z3, z3, conv_w.astype(F32), x3, w_bf)


def _ffn_kernel(x_ref, g_ref, wg_ref, wu_ref, wd_ref, o_ref, h_ref):
    f = pl.program_id(1)

    @pl.when(f == 0)
    def _():
        x = x_ref[...]
        h_ref[...] = _rms(x, g_ref[...]).astype(BF16)
        o_ref[...] = x

    h = h_ref[...]
    gate = jnp.dot(h, wg_ref[...], preferred_element_type=F32)
    up = jnp.dot(h, wu_ref[...], preferred_element_type=F32)
    act = (gate * jax.nn.sigmoid(gate) * up).astype(BF16)
    o_ref[...] += jnp.dot(act, wd_ref[...], preferred_element_type=F32)


def _ffn(x2, g, wgu_bf, wd_bf):
    t, d = x2.shape
    tm = min(512, t)
    tf = 512
    nf = FF_DENSE // tf
    return pl.pallas_call(
        _ffn_kernel,
        grid=(t // tm, nf),
        in_specs=[
            pl.BlockSpec((tm, d), lambda i, f: (i, 0)),
            pl.BlockSpec((1, d), lambda i, f: (0, 0)),
            pl.BlockSpec((d, tf), lambda i, f: (0, f)),
            pl.BlockSpec((d, tf), lambda i, f: (0, f + nf)),
            pl.BlockSpec((tf, d), lambda i, f: (f, 0)),
        ],
        out_specs=pl.BlockSpec((tm, d), lambda i, f: (i, 0)),
        out_shape=jax.ShapeDtypeStruct((t, d), F32),
        scratch_shapes=[pltpu.VMEM((tm, d), BF16)],
        compiler_params=_params(("parallel", "arbitrary")),
        name="ffn_swiglu",
    )(x2, g.reshape(1, d), wgu_bf, wgu_bf, wd_bf)


def _norm_kernel(x_ref, g_ref, o_ref):
    o_ref[...] = _rms(x_ref[...], g_ref[...]).astype(o_ref.dtype)


def _norm_bf16(x2, g):
    t, d = x2.shape
    tm = min(1024, t)
    return pl.pallas_call(
        _norm_kernel,
        grid=(t // tm,),
        in_specs=[pl.BlockSpec((tm, d), lambda i: (i, 0)), pl.BlockSpec((1, d), lambda i: (0, 0))],
        out_specs=pl.BlockSpec((tm, d), lambda i: (i, 0)),
        out_shape=jax.ShapeDtypeStruct((t, d), BF16),
        compiler_params=_params(("parallel",)),
        name="s5_norm",
    )(x2, g.reshape(1, d))


_K_BP, _K_CP, _K_BW, _K_CS = 0, 1, 2, 3
_N_KIND = 4
_SCAN_STEPS_MAX = 16


def _s5_kernel(u_ref, pt_ref, yt_ref, al_ref, pos_ref, rem_ref, o_ref, tab_ref, *, rows, n_steps):
    def build(s, carry):
        for d in range(2):
            for kind in range(_N_KIND):
                x = pt_ref[d * 2 * _N_KIND + 2 * kind, pl.ds(s, 1), :]
                xs = pt_ref[d * 2 * _N_KIND + 2 * kind + 1, pl.ds(s, 1), :]
                par = 0 if kind in (_K_BP, _K_BW) else 2
                ya = yt_ref[d * 4 + par]
                yb = yt_ref[d * 4 + par + 1]
                val = x * ya + xs * yb
                tab_ref[d * _N_KIND + kind, pl.ds(pl.multiple_of(s * S5_P, S5_P), S5_P), :] = val.astype(BF16)
        return carry

    lax.fori_loop(0, S5_CHUNK, build, 0)

    nt = (((1,), (1,)), ((), ()))
    ri = lax.broadcasted_iota(jnp.int32, (S5_ROW, S5_ROW), 0) // S5_P
    ci = lax.broadcasted_iota(jnp.int32, (S5_ROW, S5_ROW), 1) // S5_P
    t_f = lax.dot_general(tab_ref[_K_BP], tab_ref[_K_CP], nt, preferred_element_type=F32)
    t_b = lax.dot_general(tab_ref[_N_KIND + _K_BP], tab_ref[_N_KIND + _K_CP], nt, preferred_element_type=F32)
    toep = (jnp.where(ri <= ci, t_f, 0.0) + jnp.where(ri >= ci, t_b, 0.0)).astype(BF16)

    u = u_ref[...]
    y = jnp.dot(u, toep, preferred_element_type=F32)

    lane = lax.broadcasted_iota(jnp.int32, (rows, LANES), 1)
    for d in range(2):
        fwd = d == 0
        dist = pos_ref[...] if fwd else rem_ref[...]
        st = jnp.dot(u, tab_ref[d * _N_KIND + _K_BW], preferred_element_type=F32)
        for k in range(n_steps):
            step = 1 << k
            sh = pltpu.roll(st, step if fwd else rows - step, 0)
            sh = jnp.where(dist >= step, sh, 0.0)
            shs = pltpu.roll(sh, S5_N, 1)
            pa = al_ref[d * 2 * _SCAN_STEPS_MAX + k:d * 2 * _SCAN_STEPS_MAX + k + 1, :]
            pb = al_ref[d * 2 * _SCAN_STEPS_MAX + _SCAN_STEPS_MAX + k:d * 2 * _SCAN_STEPS_MAX + _SCAN_STEPS_MAX + k + 1, :]
            st = st + sh * pa + shs * pb
        inc = pltpu.roll(st, 1 if fwd else rows - 1, 0)
        inc = jnp.where(dist >= 1, inc, 0.0).astype(BF16)
        y = y + lax.dot_general(inc, tab_ref[d * _N_KIND + _K_CS], nt, preferred_element_type=F32)
    del lane
    o_ref[...] = y.astype(o_ref.dtype)


def _s5_tables(a_re, a_im, log_step, b_re, b_im, c_re, c_im, n_steps):
    dt = jnp.exp(log_step.astype(F32))[..., None]
    a_re = a_re.astype(F32)
    a_im = a_im.astype(F32)
    lr, li = a_re * dt, a_im * dt
    er = jnp.exp(lr)
    abr, abi = er * jnp.cos(li), er * jnp.sin(li)
    den = a_re * a_re + a_im * a_im
    fr = ((abr - 1.0) * a_re + abi * a_im) / den
    fi = (abi * a_re - (abr - 1.0) * a_im) / den
    b_re = b_re.astype(F32)
    b_im = b_im.astype(F32)
    bbr = fr[..., None] * b_re - fi[..., None] * b_im
    bbi = fr[..., None] * b_im + fi[..., None] * b_re
    bbr, bbi = jnp.swapaxes(bbr, -1, -2), jnp.swapaxes(bbi, -1, -2)
    cr, ci = c_re.astype(F32), c_im.astype(F32)
    cat = lambda lo, hi: jnp.concatenate([lo, hi], axis=-1)
    yt = jnp.stack([cat(bbr, bbr), cat(-bbi, bbi), cat(cr, -cr), cat(-ci, -ci)], axis=2)
    yt = jnp.swapaxes(yt, 0, 1).reshape(S5_GROUPS, 2 * 4, S5_P, LANES)

    idx = jnp.arange(S5_CHUNK, dtype=F32)
    length = float(S5_CHUNK)
    expo = jnp.stack([
        jnp.stack([-idx, idx, length - 1.0 - idx, idx + 1.0]),
        jnp.stack([idx, -idx, idx, length - idx]),
    ])
    e = expo[:, None, :, :, None]
    mag = jnp.exp(lr[:, :, None, None, :] * e)
    ang = li[:, :, None, None, :] * e
    pr, pi = mag * jnp.cos(ang), mag * jnp.sin(ang)
    pt = jnp.stack([cat(pr, pi), cat(pi, pr)], axis=3)
    pt = jnp.swapaxes(pt, 0, 1).reshape(S5_GROUPS, 2 * _N_KIND * 2, S5_CHUNK, LANES)

    steps = length * (2.0 ** jnp.arange(_SCAN_STEPS_MAX, dtype=F32))
    se = steps[None, None, :, None]
    smag = jnp.exp(lr[:, :, None, :] * se)
    sang = li[:, :, None, :] * se
    sr, si = smag * jnp.cos(sang), smag * jnp.sin(sang)
    al = jnp.concatenate([cat(sr, sr), cat(-si, si)], axis=2)
    al = jnp.swapaxes(al, 0, 1).reshape(S5_GROUPS, 2 * 2 * _SCAN_STEPS_MAX, LANES)
    del n_steps
    return pt, yt, al


def _s5(u_list, shapes, a_re, a_im, log_step, b_re, b_im, c_re, c_im):
    pos_parts, rem_parts, max_cps = [], [], 1
    for (b, s) in shapes:
        cps = s // S5_CHUNK
        max_cps = max(max_cps, cps)
        p = jnp.tile(jnp.arange(cps, dtype=jnp.int32), b)
        pos_parts.append(p)
        rem_parts.append(cps - 1 - p)
    n_steps = max(1, (max_cps - 1).bit_length())
    pos = jnp.concatenate(pos_parts)
    rem = jnp.concatenate(rem_parts)
    rows = pos.shape[0]
    pos = jnp.broadcast_to(pos[:, None], (rows, LANES))
    rem = jnp.broadcast_to(rem[:, None], (rows, LANES))
    pt, yt, al = _s5_tables(a_re, a_im, log_step, b_re, b_im, c_re, c_im, n_steps)

    u_all = jnp.concatenate(u_list, axis=0)
    u_t = u_all.reshape(rows, S5_CHUNK, S5_GROUPS, S5_P).transpose(2, 0, 1, 3).reshape(S5_GROUPS, rows, S5_ROW)
    y_t = pl.pallas_call(
        functools.partial(_s5_kernel, rows=rows, n_steps=n_steps),
        grid=(S5_GROUPS,),
        in_specs=[
            pl.BlockSpec((None, rows, S5_ROW), lambda g: (g, 0, 0)),
            pl.BlockSpec((None, 2 * _N_KIND * 2, S5_CHUNK, LANES), lambda g: (g, 0, 0, 0)),
            pl.BlockSpec((None, 2 * 4, S5_P, LANES), lambda g: (g, 0, 0, 0)),
            pl.BlockSpec((None, 2 * 2 * _SCAN_STEPS_MAX, LANES), lambda g: (g, 0, 0)),
            pl.BlockSpec((rows, LANES), lambda g: (0, 0)),
            pl.BlockSpec((rows, LANES), lambda g: (0, 0)),
        ],
        out_specs=pl.BlockSpec((None, rows, S5_ROW), lambda g: (g, 0, 0)),
        out_shape=jax.ShapeDtypeStruct((S5_GROUPS, rows, S5_ROW), BF16),
        scratch_shapes=[pltpu.VMEM((2 * _N_KIND, S5_ROW, LANES), BF16)],
        compiler_params=_params(("parallel",)),
        name="s5_chunked",
    )(u_t, pt, yt, al, pos, rem)
    y_all = y_t.reshape(S5_GROUPS, rows, S5_CHUNK, S5_P).transpose(1, 2, 0, 3).reshape(rows * S5_CHUNK, D_MODEL)
    outs, off = [], 0
    for (b, s) in shapes:
        outs.append(y_all[off:off + b * s])
        off += b * s
    return outs


def _glu_kernel(x_ref, y_ref, g_ref, d_ref, wa_ref, wb_ref, o_ref, a_ref, *, tn):
    j = pl.program_id(1)

    @pl.when(j == 0)
    def _():
        def rows(c, carry):
            r = pl.ds(pl.multiple_of(c * ROW_CHUNK, ROW_CHUNK), ROW_CHUNK)
            h = _rms(x_ref[r, :], g_ref[...])
            t = (y_ref[r, :].astype(F32) + d_ref[...] * h)
            gelu = 0.5 * t * (1.0 + jnp.tanh(math.sqrt(2.0 / math.pi) * (t + 0.044715 * (t * t * t))))
            a_ref[r, :] = gelu.astype(BF16)
            return carry
        lax.fori_loop(0, a_ref.shape[0] // ROW_CHUNK, rows, 0)

    a = a_ref[...]
    ga = jnp.dot(a, wa_ref[...], preferred_element_type=F32)
    gb = jnp.dot(a, wb_ref[...], preferred_element_type=F32)
    xs = x_ref[:, pl.ds(pl.multiple_of(j * tn, tn), tn)]
    o_ref[...] = xs + ga * jax.nn.sigmoid(gb)


def _glu(x2, y2, g, dskip, w_bf):
    t, d = x2.shape
    tm = min(1024, t)
    tn = 512
    nj = d // tn
    return pl.pallas_call(
        functools.partial(_glu_kernel, tn=tn),
        grid=(t // tm, nj),
        in_specs=[
            pl.BlockSpec((tm, d), lambda i, j: (i, 0)),
            pl.BlockSpec((tm, d), lambda i, j: (i, 0)),
            pl.BlockSpec((1, d), lambda i, j: (0, 0)),
            pl.BlockSpec((1, d), lambda i, j: (0, 0)),
            pl.BlockSpec((d, tn), lambda i, j: (0, j)),
            pl.BlockSpec((d, tn), lambda i, j: (0, j + nj)),
        ],
        out_specs=pl.BlockSpec((tm, tn), lambda i, j: (i, j)),
        out_shape=jax.ShapeDtypeStruct((t, d), F32),
        scratch_shapes=[pltpu.VMEM((tm, d), BF16)],
        compiler_params=_params(("parallel", "arbitrary")),
        name="s5_glu",
    )(x2, y2, g.reshape(1, d), dskip.reshape(1, d).astype(F32), w_bf, w_bf)


MOE_TILE = 1024
MOE_FF_TILE = 512
COMBINE_TILE = 256
DMA_UNROLL = 8
ROW_CHUNK = 128


def _route_kernel(x_ref, g_ref, wr_ref, idx_ref, gate_ref):
    h = _rms(x_ref[...], g_ref[...])
    logits = jnp.dot(h, wr_ref[...], preferred_element_type=F32, precision=lax.Precision.HIGHEST)
    lane = lax.broadcasted_iota(jnp.int32, logits.shape, 1)
    neg = jnp.float32(-jnp.inf)
    lg = jnp.where(lane < N_EXPERTS, logits, neg)
    m1 = jnp.max(lg, axis=-1, keepdims=True)
    i1 = jnp.min(jnp.where(lg == m1, lane, LANES), axis=-1, keepdims=True)
    lg2 = jnp.where(lane == i1, neg, lg)
    m2 = jnp.max(lg2, axis=-1, keepdims=True)
    i2 = jnp.min(jnp.where(lg2 == m2, lane, LANES), axis=-1, keepdims=True)
    ex = jnp.exp(m2 - m1)
    den = 1.0 + ex
    idx_ref[...] = jnp.where(lane == 0, i1, jnp.where(lane == 1, i2, 0))
    gate_ref[...] = jnp.where(lane == 0, 1.0 / den, jnp.where(lane == 1, ex / den, 0.0))


def _route(x2, g, w_router):
    t, d = x2.shape
    tm = min(512, t)
    wr = jnp.zeros((d, LANES), F32).at[:, :N_EXPERTS].set(w_router.astype(F32))
    return pl.pallas_call(
        _route_kernel,
        grid=(t // tm,),
        in_specs=[
            pl.BlockSpec((tm, d), lambda i: (i, 0)),
            pl.BlockSpec((1, d), lambda i: (0, 0)),
            pl.BlockSpec((d, LANES), lambda i: (0, 0)),
        ],
        out_specs=[pl.BlockSpec((tm, LANES), lambda i: (i, 0)), pl.BlockSpec((tm, LANES), lambda i: (i, 0))],
        out_shape=[jax.ShapeDtypeStruct((t, LANES), jnp.int32), jax.ShapeDtypeStruct((t, LANES), F32)],
        compiler_params=_params(("parallel",)),
        name="moe_route",
    )(x2, g.reshape(1, d), wr)


def _expert_kernel(src_ref, te_ref, tv_ref, x_hbm, g_ref, wg_ref, wu_ref, wd_ref, o_ref, xbuf, hb_ref, sem,
                   *, tm, n_tiles):
    i = pl.program_id(0)
    f = pl.program_id(1)

    def row_copy(tile, r):
        tok = src_ref[tile * tm + r]
        return pltpu.make_async_copy(x_hbm.at[pl.ds(tok, 1)], xbuf.at[pl.ds(r, 1)], sem.at[0])

    def start_gather(tile):
        def go(r, c):
            row_copy(tile, r).start()
            return c
        lax.fori_loop(0, tm, go, 0, unroll=DMA_UNROLL)

    def wait_gather(tile):
        def go(r, c):
            row_copy(tile, r).wait()
            return c
        lax.fori_loop(0, tm, go, 0, unroll=DMA_UNROLL)

    @pl.when(f == 0)
    def _():
        @pl.when(i == 0)
        def _():
            start_gather(0)

        wait_gather(i)
        chunk = min(ROW_CHUNK, tm)

        def rows(c, carry):
            r = pl.ds(pl.multiple_of(c * chunk, chunk), chunk)
            hb_ref[r, :] = _rms(xbuf[r, :], g_ref[...]).astype(BF16)
            return carry
        lax.fori_loop(0, tm // chunk, rows, 0)

        @pl.when(i + 1 < n_tiles)
        def _():
            start_gather(i + 1)

        o_ref[...] = jnp.zeros(o_ref.shape, F32)

    @pl.when(tv_ref[i] != 0)
    def _():
        h = hb_ref[...]
        gate = jnp.dot(h, wg_ref[...], preferred_element_type=F32)
        up = jnp.dot(h, wu_ref[...], preferred_element_type=F32)
        act = (gate * jax.nn.sigmoid(gate) * up).astype(BF16)
        o_ref[...] += jnp.dot(act, wd_ref[...], preferred_element_type=F32)


def _combine_kernel(dest_ref, x_ref, gate_ref, gf_ref, y_hbm, o_ref, ybuf, sem, *, tm, n_tiles):
    i = pl.program_id(0)
    slot = i % 2

    def row_copy(tile, r, k, buf):
        row = dest_ref[(tile * tm + r) * 2 + k]
        return pltpu.make_async_copy(y_hbm.at[pl.ds(row, 1)], ybuf.at[buf, k, pl.ds(r, 1)], sem.at[buf])

    def start_gather(tile, buf):
        def go(r, c):
            row_copy(tile, r, 0, buf).start()
            row_copy(tile, r, 1, buf).start()
            return c
        lax.fori_loop(0, tm, go, 0, unroll=DMA_UNROLL)

    def wait_gather(tile, buf):
        def go(r, c):
            row_copy(tile, r, 0, buf).wait()
            row_copy(tile, r, 1, buf).wait()
            return c
        lax.fori_loop(0, tm, go, 0, unroll=DMA_UNROLL)

    @pl.when(i == 0)
    def _():
        start_gather(0, 0)

    @pl.when(i + 1 < n_tiles)
    def _():
        start_gather(i + 1, 1 - slot)

    wait_gather(i, slot)
    gate = gate_ref[...]
    y = x_ref[...] + gate[:, 0:1] * ybuf[slot, 0] + gate[:, 1:2] * ybuf[slot, 1]
    o_ref[...] = _rms(y, gf_ref[...])


def _moe(x2, g, w_router, wgu_bf, wd_bf, g_final):
    t, d = x2.shape
    tm = MOE_TILE if t >= 8 * MOE_TILE else 64
    tf = MOE_FF_TILE
    nf = FF_EXPERT // tf
    ridx, rgate = _route(x2, g, w_router)

    n_assign = 2 * t
    expert = ridx[:, 0:2].reshape(n_assign)
    onehot = (expert[:, None] == jnp.arange(N_EXPERTS, dtype=jnp.int32)[None, :]).astype(jnp.int32)
    csum = jnp.cumsum(onehot, axis=0)
    rank = jnp.sum((csum - onehot) * onehot, axis=1)
    padded = ((csum[-1] + tm - 1) // tm) * tm
    pend = jnp.cumsum(padded)
    dest = jnp.sum(onehot * (pend - padded)[None, :], axis=1) + rank
    n_slots = n_assign + N_EXPERTS * tm
    n_tiles = n_slots // tm
    src = jnp.zeros((n_slots,), jnp.int32).at[dest].set(jnp.arange(n_assign, dtype=jnp.int32) // 2)
    tile_start = jnp.arange(n_tiles, dtype=jnp.int32) * tm
    tile_expert = jnp.minimum(jnp.sum((tile_start[:, None] >= pend[None, :]).astype(jnp.int32), axis=1),
                              N_EXPERTS - 1)
    tile_valid = (tile_start < pend[-1]).astype(jnp.int32)

    wcol = lambda i, f, tv: jnp.where(tv[i] != 0, f, nf - 1)
    ys = pl.pallas_call(
        functools.partial(_expert_kernel, tm=tm, n_tiles=n_tiles),
        grid_spec=pltpu.PrefetchScalarGridSpec(
            num_scalar_prefetch=3,
            grid=(n_tiles, nf),
            in_specs=[
                pl.BlockSpec(memory_space=pl.ANY),
                pl.BlockSpec((1, d), lambda i, f, src, te, tv: (0, 0)),
                pl.BlockSpec((None, d, tf), lambda i, f, src, te, tv: (te[i], 0, wcol(i, f, tv))),
                pl.BlockSpec((None, d, tf), lambda i, f, src, te, tv: (te[i], 0, wcol(i, f, tv) + nf)),
                pl.BlockSpec((None, tf, d), lambda i, f, src, te, tv: (te[i], wcol(i, f, tv), 0)),
            ],
            out_specs=pl.BlockSpec((tm, d), lambda i, f, src, te, tv: (i, 0)),
            scratch_shapes=[
                pltpu.VMEM((tm, d), F32),
                pltpu.VMEM((tm, d), BF16),
                pltpu.SemaphoreType.DMA((1,)),
            ],
        ),
        out_shape=jax.ShapeDtypeStruct((n_slots, d), F32),
        compiler_params=_params(("arbitrary", "arbitrary")),
        name="moe_experts",
    )(src, tile_expert, tile_valid, x2, g.reshape(1, d), wgu_bf, wgu_bf, wd_bf)

    tc = min(COMBINE_TILE, t)
    nc = t // tc
    return pl.pallas_call(
        functools.partial(_combine_kernel, tm=tc, n_tiles=nc),
        grid_spec=pltpu.PrefetchScalarGridSpec(
            num_scalar_prefetch=1,
            grid=(nc,),
            in_specs=[
                pl.BlockSpec((tc, d), lambda i, dst: (i, 0)),
                pl.BlockSpec((tc, LANES), lambda i, dst: (i, 0)),
                pl.BlockSpec((1, d), lambda i, dst: (0, 0)),
                pl.BlockSpec(memory_space=pl.ANY),
            ],
            out_specs=pl.BlockSpec((tc, d), lambda i, dst: (i, 0)),
            scratch_shapes=[
                pltpu.VMEM((2, 2, tc, d), F32),
                pltpu.SemaphoreType.DMA((2,)),
            ],
        ),
        out_shape=jax.ShapeDtypeStruct((t, d), F32),
        compiler_params=_params(("arbitrary",)),
        name="moe_combine",
    )(dest, x2, rgate, g_final.reshape(1, d), ys)


def _trunks(xs, g_mix_e, w_in_e, lambda_qk, subln_g, conv_w, w_out_e, g_ffn_e, w_gate_up, w_down,
            g_mix_o, s5_a_re, s5_a_im, s5_log_step, s5_b_re, s5_b_im, s5_c_re, s5_c_im, s5_d,
            w_glu, g_ffn_o, w_router, w_exp_gate_up, w_exp_down, g_final):
    w_in = w_in_e[0].astype(BF16)
    w_out = w_out_e[0].astype(BF16)
    w_gu = w_gate_up[0].astype(BF16)
    w_dn = w_down[0].astype(BF16)
    w_gl = w_glu[0].astype(BF16)
    w_egu = w_exp_gate_up[0].astype(BF16)
    w_edn = w_exp_down[0].astype(BF16)

    shapes = [(x.shape[0], x.shape[1]) for x in xs]
    x1s, us = [], []
    for x in xs:
        b, s, d = x.shape
        z = _in_proj(x.reshape(b * s, d), g_mix_e[0], w_in, s).reshape(b, s, IN_COLS)
        a = _attention(z, lambda_qk[0], subln_g[0])
        x1 = _out_proj(a, z, conv_w[0], x, w_out).reshape(b * s, d)
        x1 = _ffn(x1, g_ffn_e[0], w_gu, w_dn)
        x1s.append(x1)
        us.append(_norm_bf16(x1, g_mix_o[0]))
    ys = _s5(us, shapes, s5_a_re[0], s5_a_im[0], s5_log_step[0], s5_b_re[0], s5_b_im[0], s5_c_re[0], s5_c_im[0])
    outs = []
    for x1, y, (b, s) in zip(x1s, ys, shapes):
        x2 = _glu(x1, y, g_mix_o[0], s5_d[0], w_gl)
        out = _moe(x2, g_ffn_o[0], w_router[0], w_egu, w_edn, g_final)
        outs.append(out.reshape(b, s, D_MODEL))
    return outs


def kernel(x_prompt, x_sample, g_mix_e, w_in_e, lambda_qk, subln_g, conv_w, w_out_e, g_ffn_e, w_gate_up, w_down, g_mix_o, s5_a_re, s5_a_im, s5_log_step, s5_b_re, s5_b_im, s5_c_re, s5_c_im, s5_d, w_glu, g_ffn_o, w_router, w_exp_gate_up, w_exp_down, g_final):
    y_prompt, y_sample = _trunks(
        [x_prompt, x_sample], g_mix_e, w_in_e, lambda_qk, subln_g, conv_w, w_out_e, g_ffn_e, w_gate_up, w_down,
        g_mix_o, s5_a_re, s5_a_im, s5_log_step, s5_b_re, s5_b_im, s5_c_re, s5_c_im, s5_d,
        w_glu, g_ffn_o, w_router, w_exp_gate_up, w_exp_down, g_final)
    return (y_prompt, y_sample)
```

```python
import functools
import math

import jax
import jax.numpy as jnp
from jax import lax
from jax.experimental import pallas as pl
from jax.experimental.pallas import tpu as pltpu

F32 = jnp.float32
BF16 = jnp.bfloat16

D_MODEL = 2048
ATTN_WIDTH = D_MODEL // 2
CONV_CH = D_MODEL - ATTN_WIDTH
N_HEADS = 8
QK_DIM = 64
V_DIM = 2 * QK_DIM
ROPE_DIM = QK_DIM // 4
ROPE_THETA = 500000.0
IN_COLS = 3 * ATTN_WIDTH + 3 * CONV_CH
S5_P = 16
S5_GROUPS = D_MODEL // S5_P
S5_N = 64
FF_DENSE = 5632
N_EXPERTS = 8
FF_EXPERT = 7168
RMS_EPS = 1e-5
LAMBDA_INIT_0 = 0.8 - 0.6 * math.exp(-0.3 * 0)
QK_SCALE_LOG2 = QK_DIM ** -0.5 * math.log2(math.e)

LANES = 128
SUBLANES = 8
ATTN_UNROLL = 5
S5_CHUNK = 128
S5_COLS = 512
S5_ROW = S5_CHUNK * S5_P
VMEM_LIMIT = 52 * 1024 * 1024


def _params(sem):
    return pltpu.CompilerParams(dimension_semantics=sem, vmem_limit_bytes=VMEM_LIMIT)


def _rms(x, g):
    ms = jnp.mean(x * x, axis=-1, keepdims=True)
    return x * lax.rsqrt(ms + RMS_EPS) * g


def _inproj_kernel(x_ref, g_ref, w_ref, ra_ref, rm_ref, rp_ref, o_ref, h_ref, *, tn, n_q, n_rope):
    j = pl.program_id(1)

    @pl.when(j == 0)
    def _():
        h_ref[...] = _rms(x_ref[...], g_ref[...]).astype(BF16)

    acc = jnp.dot(h_ref[...], w_ref[...], preferred_element_type=F32)

    @pl.when(j < n_rope)
    def _():
        scale = jnp.where(j < n_q, QK_SCALE_LOG2, 1.0).astype(F32)
        a = ra_ref[...] * scale
        bm = rm_ref[...] * scale
        bp = rp_ref[...] * scale
        for c in range(tn // LANES):
            t = acc[:, c * LANES:(c + 1) * LANES]
            r = t * a + pltpu.roll(t, LANES - ROPE_DIM // 2, 1) * bm + pltpu.roll(t, ROPE_DIM // 2, 1) * bp
            o_ref[:, c * LANES:(c + 1) * LANES] = r.astype(o_ref.dtype)

    @pl.when(j >= n_rope)
    def _():
        o_ref[...] = acc.astype(o_ref.dtype)


def _rope_tables(seq):
    half = ROPE_DIM // 2
    inv = 1.0 / (ROPE_THETA ** (jnp.arange(0, ROPE_DIM, 2, dtype=F32) / ROPE_DIM))
    ang = jnp.arange(seq, dtype=F32)[:, None] * inv[None, :]
    cos, sin = jnp.cos(ang), jnp.sin(ang)
    ones = jnp.ones((seq, QK_DIM - ROPE_DIM), F32)
    zeros = jnp.zeros((seq, QK_DIM - ROPE_DIM), F32)
    zh = jnp.zeros((seq, half), F32)
    a = jnp.concatenate([cos, cos, ones], axis=-1)
    bm = jnp.concatenate([-sin, zh, zeros], axis=-1)
    bp = jnp.concatenate([zh, sin, zeros], axis=-1)
    rep = LANES // QK_DIM
    return jnp.tile(a, (1, rep)), jnp.tile(bm, (1, rep)), jnp.tile(bp, (1, rep))


def _in_proj(x2, g, w_bf, seq):
    t, d = x2.shape
    tm = min(1024, seq)
    tn = 1024
    ra, rm, rp = _rope_tables(seq)
    nblk = seq // tm
    rope_spec = pl.BlockSpec((tm, LANES), lambda i, j: (i % nblk, 0))
    return pl.pallas_call(
        functools.partial(_inproj_kernel, tn=tn, n_q=ATTN_WIDTH // tn, n_rope=2 * ATTN_WIDTH // tn),
        grid=(t // tm, IN_COLS // tn),
        in_specs=[
            pl.BlockSpec((tm, d), lambda i, j: (i, 0)),
            pl.BlockSpec((1, d), lambda i, j: (0, 0)),
            pl.BlockSpec((d, tn), lambda i, j: (0, j)),
            rope_spec, rope_spec, rope_spec,
        ],
        out_specs=pl.BlockSpec((tm, tn), lambda i, j: (i, j)),
        out_shape=jax.ShapeDtypeStruct((t, IN_COLS), BF16),
        scratch_shapes=[pltpu.VMEM((tm, d), BF16)],
        compiler_params=_params(("parallel", "arbitrary")),
        name="in_proj",
    )(x2, g.reshape(1, d), w_bf, ra, rm, rp)


def _attn_kernel(q_ref, k_ref, v_ref, lq_ref, sg_ref, o_ref, qq_ref, vt_ref, s0_ref, s1_ref, acc_ref,
                 *, tq, tk, nk):
    nq = 2 * tq

    @pl.when(pl.program_id(2) == 0)
    def _():
        def flip(c, carry):
            start = pl.multiple_of(c * tk, tk)
            vt_ref[:, pl.ds(start, tk)] = v_ref[pl.ds(start, tk), :].astype(F32).T.astype(BF16)
            return carry
        lax.fori_loop(0, nk, flip, 0)

    qt = q_ref[...].astype(F32).T.astype(BF16)
    dim = lax.broadcasted_iota(jnp.int32, (V_DIM, tq), 0)
    zero = jnp.zeros_like(qt)
    qq_ref[:, 0:tq] = jnp.where(dim < QK_DIM, qt, zero)
    qq_ref[:, tq:nq] = jnp.where(dim >= QK_DIM, qt, zero)
    acc_ref[...] = jnp.zeros(acc_ref.shape, F32)

    def scores(j, s_ref):
        start = pl.multiple_of(j * tk, tk)
        s = jnp.dot(k_ref[pl.ds(start, tk), :], qq_ref[...], preferred_element_type=F32)
        s_ref[...] = s
        return jnp.max(s, axis=0, keepdims=True)

    def accumulate(j, s_ref, carry, block_max):
        m_prev, l_prev = carry
        m_new = jnp.maximum(m_prev, block_max)
        alpha = jnp.exp2(m_prev - m_new)
        p = jnp.exp2(s_ref[...] - m_new)
        l_new = alpha * l_prev + jnp.sum(p, axis=0, keepdims=True)
        start = pl.multiple_of(j * tk, tk)
        pv = jnp.dot(vt_ref[:, pl.ds(start, tk)], p.astype(BF16), preferred_element_type=F32)
        acc_ref[...] = alpha * acc_ref[...] + pv
        return m_new, l_new

    def body(jj, carry):
        ml, max_even = carry
        j = 2 * jj
        max_odd = scores(j + 1, s1_ref)
        ml = accumulate(j, s0_ref, ml, max_even)
        max_even = scores(j + 2, s0_ref)
        ml = accumulate(j + 1, s1_ref, ml, max_odd)
        return ml, max_even

    ml0 = (jnp.full((1, nq), -jnp.inf, F32), jnp.zeros((1, nq), F32))
    ml, max_even = lax.fori_loop(0, nk // 2 - 1, body, (ml0, scores(0, s0_ref)), unroll=ATTN_UNROLL)
    max_odd = scores(nk - 1, s1_ref)
    ml = accumulate(nk - 2, s0_ref, ml, max_even)
    _, denom = accumulate(nk - 1, s1_ref, ml, max_odd)

    lq = lq_ref[...]
    lam = (jnp.exp(jnp.sum(lq[0:1, :] * lq[1:2, :], axis=-1, keepdims=True))
           - jnp.exp(jnp.sum(lq[2:3, :] * lq[3:4, :], axis=-1, keepdims=True)) + LAMBDA_INIT_0)
    ot = acc_ref[...] / denom
    diff = ot[:, 0:tq] - lam * ot[:, tq:nq]
    ms = jnp.mean(diff * diff, axis=0, keepdims=True)
    y = diff * lax.rsqrt(ms + RMS_EPS) * sg_ref[...] * (1.0 - LAMBDA_INIT_0)
    o_ref[...] = y.T.astype(o_ref.dtype)


def _attention(z3, lambda_qk, subln_g):
    b, s, _ = z3.shape
    tq = min(512, s)
    tk = min(512, s // 2)
    assert s % tq == 0 and s % (2 * tk) == 0, "key chunks are processed in pairs"
    kcol = ATTN_WIDTH // V_DIM
    return pl.pallas_call(
        functools.partial(_attn_kernel, tq=tq, tk=tk, nk=s // tk),
        grid=(b, N_HEADS, s // tq),
        in_specs=[
            pl.BlockSpec((None, tq, V_DIM), lambda bi, h, i: (bi, i, h)),
            pl.BlockSpec((None, s, V_DIM), lambda bi, h, i: (bi, 0, kcol + h)),
            pl.BlockSpec((None, s, V_DIM), lambda bi, h, i: (bi, 0, 2 * kcol + h)),
            pl.BlockSpec((4, QK_DIM), lambda bi, h, i: (0, 0)),
            pl.BlockSpec((V_DIM, 1), lambda bi, h, i: (0, 0)),
        ],
        out_specs=pl.BlockSpec((None, tq, V_DIM), lambda bi, h, i: (bi, i, h)),
        out_shape=jax.ShapeDtypeStruct((b, s, ATTN_WIDTH), BF16),
        scratch_shapes=[
            pltpu.VMEM((V_DIM, 2 * tq), BF16),
            pltpu.VMEM((V_DIM, s), BF16),
            pltpu.VMEM((tk, 2 * tq), F32),
            pltpu.VMEM((tk, 2 * tq), F32),
            pltpu.VMEM((V_DIM, 2 * tq), F32),
        ],
        compiler_params=_params(("parallel", "parallel", "arbitrary")),
        name="diff_attention",
    )(z3, z3, z3, lambda_qk.astype(F32), subln_g.reshape(V_DIM, 1).astype(F32))


def _outproj_kernel(a_ref, bg_ref, cg_ref, xc_ref, cgp_ref, xcp_ref, cgn_ref, xcn_ref, cw_ref, x_ref, w_ref,
                    o_ref, *, tm, nt):
    i = pl.program_id(1)
    u = cg_ref[...].astype(F32) * xc_ref[...].astype(F32)
    u_before = cgp_ref[SUBLANES - 1:SUBLANES, :].astype(F32) * xcp_ref[SUBLANES - 1:SUBLANES, :].astype(F32)
    u_after = cgn_ref[0:1, :].astype(F32) * xcn_ref[0:1, :].astype(F32)
    u_before = u_before * (i > 0).astype(F32)
    u_after = u_after * (i < nt - 1).astype(F32)
    row = lax.broadcasted_iota(jnp.int32, u.shape, 0)
    u_prev = jnp.where(row == 0, u_before, pltpu.roll(u, 1, 0))
    u_next = jnp.where(row == tm - 1, u_after, pltpu.roll(u, tm - 1, 0))
    cw = cw_ref[...]
    y = u_prev * cw[0:1, :] + u * cw[1:2, :] + u_next * cw[2:3, :]
    c = (bg_ref[...].astype(F32) * y).astype(BF16)
    acc = jnp.dot(a_ref[...], w_ref[0:ATTN_WIDTH, :], preferred_element_type=F32)
    acc = acc + jnp.dot(c, w_ref[ATTN_WIDTH:D_MODEL, :], preferred_element_type=F32)
    o_ref[...] = x_ref[...] + acc


def _out_proj(a3, z3, conv_w, x3, w_bf):
    b, s, _ = z3.shape
    tm = min(512, s)
    nt = s // tm
    rb = tm // SUBLANES
    last = s // SUBLANES - 1
    cb = 3 * ATTN_WIDTH // CONV_CH
    main = lambda col: pl.BlockSpec((None, tm, CONV_CH), lambda bi, i: (bi, i, col))
    prev = lambda col: pl.BlockSpec((None, SUBLANES, CONV_CH), lambda bi, i: (bi, jnp.maximum(i * rb - 1, 0), col))
    nxt = lambda col: pl.BlockSpec((None, SUBLANES, CONV_CH), lambda bi, i: (bi, jnp.minimum((i + 1) * rb, last), col))
    return pl.pallas_call(
        functools.partial(_outproj_kernel, tm=tm, nt=nt),
        grid=(b, nt),
        in_specs=[
            pl.BlockSpec((None, tm, ATTN_WIDTH), lambda bi, i: (bi, i, 0)),
            main(cb), main(cb + 1), main(cb + 2),
            prev(cb + 1), prev(cb + 2), nxt(cb + 1), nxt(cb + 2),
            pl.BlockSpec((3, CONV_CH), lambda bi, i: (0, 0)),
            pl.BlockSpec((None, tm, D_MODEL), lambda bi, i: (bi, i, 0)),
            pl.BlockSpec((D_MODEL, D_MODEL), lambda bi, i: (0, 0)),
        ],
        out_specs=pl.BlockSpec((None, tm, D_MODEL), lambda bi, i: (bi, i, 0)),
        out_shape=jax.ShapeDtypeStruct((b, s, D_MODEL), F32),
        compiler_params=_params(("parallel", "arbitrary")),
        name="out_proj_conv",
    )(a3, z3, z3, z3, z3, z3, z3, z3, conv_w.astype(F32), x3, w_bf)


def _ffn_kernel(x_ref, g_ref, wg_ref, wu_ref, wd_ref, o_ref, h_ref):
    f = pl.program_id(1)

    @pl.when(f == 0)
    def _():
        x = x_ref[...]
        h_ref[...] = _rms(x, g_ref[...]).astype(BF16)
        o_ref[...] = x

    h = h_ref[...]
    gate = jnp.dot(h, wg_ref[...], preferred_element_type=F32)
    up = jnp.dot(h, wu_ref[...], preferred_element_type=F32)
    act = (gate * jax.nn.sigmoid(gate) * up).astype(BF16)
    o_ref[...] += jnp.dot(act, wd_ref[...], preferred_element_type=F32)


def _ffn(x2, g, wgu_bf, wd_bf):
    t, d = x2.shape
    tm = min(512, t)
    tf = 512
    nf = FF_DENSE // tf
    return pl.pallas_call(
        _ffn_kernel,
        grid=(t // tm, nf),
        in_specs=[
            pl.BlockSpec((tm, d), lambda i, f: (i, 0)),
            pl.BlockSpec((1, d), lambda i, f: (0, 0)),
            pl.BlockSpec((d, tf), lambda i, f: (0, f)),
            pl.BlockSpec((d, tf), lambda i, f: (0, f + nf)),
            pl.BlockSpec((tf, d), lambda i, f: (f, 0)),
        ],
        out_specs=pl.BlockSpec((tm, d), lambda i, f: (i, 0)),
        out_shape=jax.ShapeDtypeStruct((t, d), F32),
        scratch_shapes=[pltpu.VMEM((tm, d), BF16)],
        compiler_params=_params(("parallel", "arbitrary")),
        name="ffn_swiglu",
    )(x2, g.reshape(1, d), wgu_bf, wgu_bf, wd_bf)


def _norm_kernel(x_ref, g_ref, o_ref):
    def rows(c, carry):
        r = pl.ds(pl.multiple_of(c * ROW_CHUNK, ROW_CHUNK), ROW_CHUNK)
        o_ref[:, r] = _rms(x_ref[r, :], g_ref[...]).T.astype(o_ref.dtype)
        return carry
    lax.fori_loop(0, x_ref.shape[0] // ROW_CHUNK, rows, 0)


def _norm_t(x2, g):
    t, d = x2.shape
    tm = min(512, t)
    return pl.pallas_call(
        _norm_kernel,
        grid=(t // tm,),
        in_specs=[pl.BlockSpec((tm, d), lambda i: (i, 0)), pl.BlockSpec((1, d), lambda i: (0, 0))],
        out_specs=pl.BlockSpec((d, tm), lambda i: (0, i)),
        out_shape=jax.ShapeDtypeStruct((d, t), BF16),
        compiler_params=_params(("parallel",)),
        name="s5_norm",
    )(x2, g.reshape(1, d))


_K_BP, _K_CP, _K_BW, _K_CS = 0, 1, 2, 3
_N_KIND = 4
_SCAN_STEPS_MAX = 16


def _s5_kernel(*refs, rows_per, n_steps):
    n_in = len(rows_per)
    u_refs = refs[:n_in]
    pt_ref, yt_ref, al_ref, pos_ref, rem_ref = refs[n_in:n_in + 5]
    o_refs = refs[n_in + 5:2 * n_in + 5]
    tab_ref, toep_ref = refs[2 * n_in + 5:]
    rows = sum(rows_per)

    for d in range(2):
        for kind in range(_N_KIND):
            x = pt_ref[d * 2 * _N_KIND + 2 * kind]
            xs = pt_ref[d * 2 * _N_KIND + 2 * kind + 1]
            par = 0 if kind in (_K_BP, _K_BW) else 2
            for q in range(S5_P):
                ya = yt_ref[d * 4 + par, q:q + 1, :]
                yb = yt_ref[d * 4 + par + 1, q:q + 1, :]
                tab_ref[d * _N_KIND + kind, q * S5_CHUNK:(q + 1) * S5_CHUNK, :] = (x * ya + xs * yb).astype(BF16)

    nt = (((1,), (1,)), ((), ()))
    s_in = lax.broadcasted_iota(jnp.int32, (S5_ROW, S5_COLS), 0) & (S5_CHUNK - 1)
    t_out = lax.broadcasted_iota(jnp.int32, (S5_ROW, S5_COLS), 1) & (S5_CHUNK - 1)
    for c in range(S5_ROW // S5_COLS):
        cols = slice(c * S5_COLS, (c + 1) * S5_COLS)
        t_f = lax.dot_general(tab_ref[_K_BP], tab_ref[_K_CP, cols, :], nt, preferred_element_type=F32)
        t_b = lax.dot_general(tab_ref[_N_KIND + _K_BP], tab_ref[_N_KIND + _K_CP, cols, :], nt,
                              preferred_element_type=F32)
        toep_ref[:, cols] = (jnp.where(s_in <= t_out, t_f, 0.0) + jnp.where(s_in >= t_out, t_b, 0.0)).astype(BF16)

    u = jnp.concatenate([jnp.concatenate([u_ref[q] for u_ref in u_refs], axis=0) for q in range(S5_P)], axis=1)
    half = S5_ROW // 2
    y = jnp.concatenate([jnp.dot(u, toep_ref[:, 0:half], preferred_element_type=F32),
                         jnp.dot(u, toep_ref[:, half:S5_ROW], preferred_element_type=F32)], axis=1)

    for d in range(2):
        fwd = d == 0
        dist = pos_ref[...] if fwd else rem_ref[...]
        st = jnp.dot(u, tab_ref[d * _N_KIND + _K_BW], preferred_element_type=F32)
        for k in range(n_steps):
            step = 1 << k
            sh = pltpu.roll(st, step if fwd else rows - step, 0)
            sh = jnp.where(dist >= step, sh, 0.0)
            shs = pltpu.roll(sh, S5_N, 1)
            pa = al_ref[d * 2 * _SCAN_STEPS_MAX + k:d * 2 * _SCAN_STEPS_MAX + k + 1, :]
            pb = al_ref[d * 2 * _SCAN_STEPS_MAX + _SCAN_STEPS_MAX + k:d * 2 * _SCAN_STEPS_MAX + _SCAN_STEPS_MAX + k + 1, :]
            st = st + sh * pa + shs * pb
        inc = pltpu.roll(st, 1 if fwd else rows - 1, 0)
        inc = jnp.where(dist >= 1, inc, 0.0).astype(BF16)
        y = y + lax.dot_general(inc, tab_ref[d * _N_KIND + _K_CS], nt, preferred_element_type=F32)

    off = 0
    for o_ref, r in zip(o_refs, rows_per):
        for p in range(S5_P):
            o_ref[p] = y[off:off + r, p * S5_CHUNK:(p + 1) * S5_CHUNK].astype(o_ref.dtype)
        off += r


def _s5_tables(a_re, a_im, log_step, b_re, b_im, c_re, c_im, n_steps):
    dt = jnp.exp(log_step.astype(F32))[..., None]
    a_re = a_re.astype(F32)
    a_im = a_im.astype(F32)
    lr, li = a_re * dt, a_im * dt
    er = jnp.exp(lr)
    abr, abi = er * jnp.cos(li), er * jnp.sin(li)
    den = a_re * a_re + a_im * a_im
    fr = ((abr - 1.0) * a_re + abi * a_im) / den
    fi = (abi * a_re - (abr - 1.0) * a_im) / den
    b_re = b_re.astype(F32)
    b_im = b_im.astype(F32)
    bbr = fr[..., None] * b_re - fi[..., None] * b_im
    bbi = fr[..., None] * b_im + fi[..., None] * b_re
    bbr, bbi = jnp.swapaxes(bbr, -1, -2), jnp.swapaxes(bbi, -1, -2)
    cr, ci = c_re.astype(F32), c_im.astype(F32)
    cat = lambda lo, hi: jnp.concatenate([lo, hi], axis=-1)
    yt = jnp.stack([cat(bbr, bbr), cat(-bbi, bbi), cat(cr, -cr), cat(-ci, -ci)], axis=2)
    yt = jnp.swapaxes(yt, 0, 1).reshape(S5_GROUPS, 2 * 4, S5_P, LANES)

    idx = jnp.arange(S5_CHUNK, dtype=F32)
    length = float(S5_CHUNK)
    expo = jnp.stack([
        jnp.stack([-idx, idx, length - 1.0 - idx, idx + 1.0]),
        jnp.stack([idx, -idx, idx, length - idx]),
    ])
    e = expo[:, None, :, :, None]
    mag = jnp.exp(lr[:, :, None, None, :] * e)
    ang = li[:, :, None, None, :] * e
    pr, pi = mag * jnp.cos(ang), mag * jnp.sin(ang)
    pt = jnp.stack([cat(pr, pi), cat(pi, pr)], axis=3)
    pt = jnp.swapaxes(pt, 0, 1).reshape(S5_GROUPS, 2 * _N_KIND * 2, S5_CHUNK, LANES)

    steps = length * (2.0 ** jnp.arange(_SCAN_STEPS_MAX, dtype=F32))
    se = steps[None, None, :, None]
    smag = jnp.exp(lr[:, :, None, :] * se)
    sang = li[:, :, None, :] * se
    sr, si = smag * jnp.cos(sang), smag * jnp.sin(sang)
    al = jnp.concatenate([cat(sr, sr), cat(-si, si)], axis=2)
    al = jnp.swapaxes(al, 0, 1).reshape(S5_GROUPS, 2 * 2 * _SCAN_STEPS_MAX, LANES)
    del n_steps
    return pt, yt, al


def _s5(ut_list, shapes, a_re, a_im, log_step, b_re, b_im, c_re, c_im):
    pos_parts, rem_parts, rows_per, max_cps = [], [], [], 1
    for (b, s) in shapes:
        cps = s // S5_CHUNK
        max_cps = max(max_cps, cps)
        p = jnp.tile(jnp.arange(cps, dtype=jnp.int32), b)
        pos_parts.append(p)
        rem_parts.append(cps - 1 - p)
        rows_per.append(b * cps)
    n_steps = max(1, (max_cps - 1).bit_length())
    rows = sum(rows_per)
    pos = jnp.broadcast_to(jnp.concatenate(pos_parts)[:, None], (rows, LANES))
    rem = jnp.broadcast_to(jnp.concatenate(rem_parts)[:, None], (rows, LANES))
    pt, yt, al = _s5_tables(a_re, a_im, log_step, b_re, b_im, c_re, c_im, n_steps)

    u4 = [ut.reshape(S5_GROUPS, S5_P, r, S5_CHUNK) for ut, r in zip(ut_list, rows_per)]
    io_specs = [pl.BlockSpec((None, S5_P, r, S5_CHUNK), lambda g: (g, 0, 0, 0)) for r in rows_per]
    outs = pl.pallas_call(
        functools.partial(_s5_kernel, rows_per=tuple(rows_per), n_steps=n_steps),
        grid=(S5_GROUPS,),
        in_specs=io_specs + [
            pl.BlockSpec((None, 2 * _N_KIND * 2, S5_CHUNK, LANES), lambda g: (g, 0, 0, 0)),
            pl.BlockSpec((None, 2 * 4, S5_P, LANES), lambda g: (g, 0, 0, 0)),
            pl.BlockSpec((None, 2 * 2 * _SCAN_STEPS_MAX, LANES), lambda g: (g, 0, 0)),
            pl.BlockSpec((rows, LANES), lambda g: (0, 0)),
            pl.BlockSpec((rows, LANES), lambda g: (0, 0)),
        ],
        out_specs=io_specs,
        out_shape=[jax.ShapeDtypeStruct((S5_GROUPS, S5_P, r, S5_CHUNK), BF16) for r in rows_per],
        scratch_shapes=[pltpu.VMEM((2 * _N_KIND, S5_ROW, LANES), BF16), pltpu.VMEM((S5_ROW, S5_ROW), BF16)],
        compiler_params=_params(("parallel",)),
        name="s5_chunked",
    )(*u4, pt, yt, al, pos, rem)
    return [o.reshape(D_MODEL, r * S5_CHUNK) for o, r in zip(outs, rows_per)]


def _glu_kernel(x_ref, yt_ref, g_ref, d_ref, wa_ref, wb_ref, o_ref, a_ref, *, tn):
    j = pl.program_id(1)

    @pl.when(j == 0)
    def _():
        def rows(c, carry):
            r = pl.ds(pl.multiple_of(c * ROW_CHUNK, ROW_CHUNK), ROW_CHUNK)
            h = _rms(x_ref[r, :], g_ref[...])
            t = (yt_ref[:, r].astype(F32).T + d_ref[...] * h)
            gelu = 0.5 * t * (1.0 + jnp.tanh(math.sqrt(2.0 / math.pi) * (t + 0.044715 * (t * t * t))))
            a_ref[r, :] = gelu.astype(BF16)
            return carry
        lax.fori_loop(0, a_ref.shape[0] // ROW_CHUNK, rows, 0)

    a = a_ref[...]
    ga = jnp.dot(a, wa_ref[...], preferred_element_type=F32)
    gb = jnp.dot(a, wb_ref[...], preferred_element_type=F32)
    xs = x_ref[:, pl.ds(pl.multiple_of(j * tn, tn), tn)]
    o_ref[...] = xs + ga * jax.nn.sigmoid(gb)


def _glu(x2, yt, g, dskip, w_bf):
    t, d = x2.shape
    tm = min(1024, t)
    tn = 512
    nj = d // tn
    return pl.pallas_call(
        functools.partial(_glu_kernel, tn=tn),
        grid=(t // tm, nj),
        in_specs=[
            pl.BlockSpec((tm, d), lambda i, j: (i, 0)),
            pl.BlockSpec((d, tm), lambda i, j: (0, i)),
            pl.BlockSpec((1, d), lambda i, j: (0, 0)),
            pl.BlockSpec((1, d), lambda i, j: (0, 0)),
            pl.BlockSpec((d, tn), lambda i, j: (0, j)),
            pl.BlockSpec((d, tn), lambda i, j: (0, j + nj)),
        ],
        out_specs=pl.BlockSpec((tm, tn), lambda i, j: (i, j)),
        out_shape=jax.ShapeDtypeStruct((t, d), F32),
        scratch_shapes=[pltpu.VMEM((tm, d), BF16)],
        compiler_params=_params(("parallel", "arbitrary")),
        name="s5_glu",
    )(x2, yt, g.reshape(1, d), dskip.reshape(1, d).astype(F32), w_bf, w_bf)


MOE_TILE = 1024
MOE_FF_TILE = 512
COMBINE_TILE = 256
DMA_UNROLL = 8
ROW_CHUNK = 128


def _route_kernel(x_ref, g_ref, wr_ref, idx_ref, gate_ref):
    h = _rms(x_ref[...], g_ref[...])
    logits = jnp.dot(h, wr_ref[...], preferred_element_type=F32, precision=lax.Precision.HIGHEST)
    lane = lax.broadcasted_iota(jnp.int32, logits.shape, 1)
    neg = jnp.float32(-jnp.inf)
    lg = jnp.where(lane < N_EXPERTS, logits, neg)
    m1 = jnp.max(lg, axis=-1, keepdims=True)
    i1 = jnp.min(jnp.where(lg == m1, lane, LANES), axis=-1, keepdims=True)
    lg2 = jnp.where(lane == i1, neg, lg)
    m2 = jnp.max(lg2, axis=-1, keepdims=True)
    i2 = jnp.min(jnp.where(lg2 == m2, lane, LANES), axis=-1, keepdims=True)
    ex = jnp.exp(m2 - m1)
    den = 1.0 + ex
    idx_ref[...] = jnp.where(lane == 0, i1, jnp.where(lane == 1, i2, 0))
    gate_ref[...] = jnp.where(lane == 0, 1.0 / den, jnp.where(lane == 1, ex / den, 0.0))


def _route(x2, g, w_router):
    t, d = x2.shape
    tm = min(512, t)
    wr = jnp.zeros((d, LANES), F32).at[:, :N_EXPERTS].set(w_router.astype(F32))
    return pl.pallas_call(
        _route_kernel,
        grid=(t // tm,),
        in_specs=[
            pl.BlockSpec((tm, d), lambda i: (i, 0)),
            pl.BlockSpec((1, d), lambda i: (0, 0)),
            pl.BlockSpec((d, LANES), lambda i: (0, 0)),
        ],
        out_specs=[pl.BlockSpec((tm, LANES), lambda i: (i, 0)), pl.BlockSpec((tm, LANES), lambda i: (i, 0))],
        out_shape=[jax.ShapeDtypeStruct((t, LANES), jnp.int32), jax.ShapeDtypeStruct((t, LANES), F32)],
        compiler_params=_params(("parallel",)),
        name="moe_route",
    )(x2, g.reshape(1, d), wr)


def _expert_kernel(src_ref, te_ref, tv_ref, x_hbm, g_ref, wg_ref, wu_ref, wd_ref, o_ref, xbuf, hb_ref, sem,
                   *, tm, n_tiles):
    i = pl.program_id(0)
    f = pl.program_id(1)

    def row_copy(tile, r):
        tok = src_ref[tile * tm + r]
        return pltpu.make_async_copy(x_hbm.at[pl.ds(tok, 1)], xbuf.at[pl.ds(r, 1)], sem.at[0])

    def start_gather(tile):
        def go(r, c):
            row_copy(tile, r).start()
            return c
        lax.fori_loop(0, tm, go, 0, unroll=DMA_UNROLL)

    def wait_gather(tile):
        def go(r, c):
            row_copy(tile, r).wait()
            return c
        lax.fori_loop(0, tm, go, 0, unroll=DMA_UNROLL)

    @pl.when(f == 0)
    def _():
        @pl.when(i == 0)
        def _():
            start_gather(0)

        wait_gather(i)
        chunk = min(ROW_CHUNK, tm)

        def rows(c, carry):
            r = pl.ds(pl.multiple_of(c * chunk, chunk), chunk)
            hb_ref[r, :] = _rms(xbuf[r, :], g_ref[...]).astype(BF16)
            return carry
        lax.fori_loop(0, tm // chunk, rows, 0)

        @pl.when(i + 1 < n_tiles)
        def _():
            start_gather(i + 1)

        o_ref[...] = jnp.zeros(o_ref.shape, F32)

    @pl.when(tv_ref[i] != 0)
    def _():
        h = hb_ref[...]
        gate = jnp.dot(h, wg_ref[...], preferred_element_type=F32)
        up = jnp.dot(h, wu_ref[...], preferred_element_type=F32)
        act = (gate * jax.nn.sigmoid(gate) * up).astype(BF16)
        o_ref[...] += jnp.dot(act, wd_ref[...], preferred_element_type=F32)


def _combine_kernel(dest_ref, x_ref, gate_ref, gf_ref, y_hbm, o_ref, ybuf, sem, *, tm, n_tiles):
    i = pl.program_id(0)
    slot = i % 2

    def row_copy(tile, r, k, buf):
        row = dest_ref[(tile * tm + r) * 2 + k]
        return pltpu.make_async_copy(y_hbm.at[pl.ds(row, 1)], ybuf.at[buf, k, pl.ds(r, 1)], sem.at[buf])

    def start_gather(tile, buf):
        def go(r, c):
            row_copy(tile, r, 0, buf).start()
            row_copy(tile, r, 1, buf).start()
            return c
        lax.fori_loop(0, tm, go, 0, unroll=DMA_UNROLL)

    def wait_gather(tile, buf):
        def go(r, c):
            row_copy(tile, r, 0, buf).wait()
            row_copy(tile, r, 1, buf).wait()
            return c
        lax.fori_loop(0, tm, go, 0, unroll=DMA_UNROLL)

    @pl.when(i == 0)
    def _():
        start_gather(0, 0)

    @pl.when(i + 1 < n_tiles)
    def _():
        start_gather(i + 1, 1 - slot)

    wait_gather(i, slot)
    gate = gate_ref[...]
    y = x_ref[...] + gate[:, 0:1] * ybuf[slot, 0] + gate[:, 1:2] * ybuf[slot, 1]
    o_ref[...] = _rms(y, gf_ref[...])


def _moe(x2, g, w_router, wgu_bf, wd_bf, g_final):
    t, d = x2.shape
    tm = MOE_TILE if t >= 8 * MOE_TILE else 64
    tf = MOE_FF_TILE
    nf = FF_EXPERT // tf
    ridx, rgate = _route(x2, g, w_router)

    n_assign = 2 * t
    expert = ridx[:, 0:2].reshape(n_assign)
    onehot = (expert[:, None] == jnp.arange(N_EXPERTS, dtype=jnp.int32)[None, :]).astype(jnp.int32)
    csum = jnp.cumsum(onehot, axis=0)
    rank = jnp.sum((csum - onehot) * onehot, axis=1)
    padded = ((csum[-1] + tm - 1) // tm) * tm
    pend = jnp.cumsum(padded)
    dest = jnp.sum(onehot * (pend - padded)[None, :], axis=1) + rank
    n_slots = n_assign + N_EXPERTS * tm
    n_tiles = n_slots // tm
    src = jnp.zeros((n_slots,), jnp.int32).at[dest].set(jnp.arange(n_assign, dtype=jnp.int32) // 2)
    tile_start = jnp.arange(n_tiles, dtype=jnp.int32) * tm
    tile_expert = jnp.minimum(jnp.sum((tile_start[:, None] >= pend[None, :]).astype(jnp.int32), axis=1),
                              N_EXPERTS - 1)
    tile_valid = (tile_start < pend[-1]).astype(jnp.int32)

    wcol = lambda i, f, tv: jnp.where(tv[i] != 0, f, nf - 1)
    ys = pl.pallas_call(
        functools.partial(_expert_kernel, tm=tm, n_tiles=n_tiles),
        grid_spec=pltpu.PrefetchScalarGridSpec(
            num_scalar_prefetch=3,
            grid=(n_tiles, nf),
            in_specs=[
                pl.BlockSpec(memory_space=pl.ANY),
                pl.BlockSpec((1, d), lambda i, f, src, te, tv: (0, 0)),
                pl.BlockSpec((None, d, tf), lambda i, f, src, te, tv: (te[i], 0, wcol(i, f, tv))),
                pl.BlockSpec((None, d, tf), lambda i, f, src, te, tv: (te[i], 0, wcol(i, f, tv) + nf)),
                pl.BlockSpec((None, tf, d), lambda i, f, src, te, tv: (te[i], wcol(i, f, tv), 0)),
            ],
            out_specs=pl.BlockSpec((tm, d), lambda i, f, src, te, tv: (i, 0)),
            scratch_shapes=[
                pltpu.VMEM((tm, d), F32),
                pltpu.VMEM((tm, d), BF16),
                pltpu.SemaphoreType.DMA((1,)),
            ],
        ),
        out_shape=jax.ShapeDtypeStruct((n_slots, d), F32),
        compiler_params=_params(("arbitrary", "arbitrary")),
        name="moe_experts",
    )(src, tile_expert, tile_valid, x2, g.reshape(1, d), wgu_bf, wgu_bf, wd_bf)

    tc = min(COMBINE_TILE, t)
    nc = t // tc
    return pl.pallas_call(
        functools.partial(_combine_kernel, tm=tc, n_tiles=nc),
        grid_spec=pltpu.PrefetchScalarGridSpec(
            num_scalar_prefetch=1,
            grid=(nc,),
            in_specs=[
                pl.BlockSpec((tc, d), lambda i, dst: (i, 0)),
                pl.BlockSpec((tc, LANES), lambda i, dst: (i, 0)),
                pl.BlockSpec((1, d), lambda i, dst: (0, 0)),
                pl.BlockSpec(memory_space=pl.ANY),
            ],
            out_specs=pl.BlockSpec((tc, d), lambda i, dst: (i, 0)),
            scratch_shapes=[
                pltpu.VMEM((2, 2, tc, d), F32),
                pltpu.SemaphoreType.DMA((2,)),
            ],
        ),
        out_shape=jax.ShapeDtypeStruct((t, d), F32),
        compiler_params=_params(("arbitrary",)),
        name="moe_combine",
    )(dest, x2, rgate, g_final.reshape(1, d), ys)


def _trunks(xs, g_mix_e, w_in_e, lambda_qk, subln_g, conv_w, w_out_e, g_ffn_e, w_gate_up, w_down,
            g_mix_o, s5_a_re, s5_a_im, s5_log_step, s5_b_re, s5_b_im, s5_c_re, s5_c_im, s5_d,
            w_glu, g_ffn_o, w_router, w_exp_gate_up, w_exp_down, g_final):
    w_in = w_in_e[0].astype(BF16)
    w_out = w_out_e[0].astype(BF16)
    w_gu = w_gate_up[0].astype(BF16)
    w_dn = w_down[0].astype(BF16)
    w_gl = w_glu[0].astype(BF16)
    w_egu = w_exp_gate_up[0].astype(BF16)
    w_edn = w_exp_down[0].astype(BF16)

    shapes = [(x.shape[0], x.shape[1]) for x in xs]
    x1s, us = [], []
    for x in xs:
        b, s, d = x.shape
        z = _in_proj(x.reshape(b * s, d), g_mix_e[0], w_in, s).reshape(b, s, IN_COLS)
        a = _attention(z, lambda_qk[0], subln_g[0])
        x1 = _out_proj(a, z, conv_w[0], x, w_out).reshape(b * s, d)
        x1 = _ffn(x1, g_ffn_e[0], w_gu, w_dn)
        x1s.append(x1)
        us.append(_norm_t(x1, g_mix_o[0]))
    ys = _s5(us, shapes, s5_a_re[0], s5_a_im[0], s5_log_step[0], s5_b_re[0], s5_b_im[0], s5_c_re[0], s5_c_im[0])
    outs = []
    for x1, y, (b, s) in zip(x1s, ys, shapes):
        x2 = _glu(x1, y, g_mix_o[0], s5_d[0], w_gl)
        out = _moe(x2, g_ffn_o[0], w_router[0], w_egu, w_edn, g_final)
        outs.append(out.reshape(b, s, D_MODEL))
    return outs


def kernel(x_prompt, x_sample, g_mix_e, w_in_e, lambda_qk, subln_g, conv_w, w_out_e, g_ffn_e, w_gate_up, w_down, g_mix_o, s5_a_re, s5_a_im, s5_log_step, s5_b_re, s5_b_im, s5_c_re, s5_c_im, s5_d, w_glu, g_ffn_o, w_router, w_exp_gate_up, w_exp_down, g_final):
    y_prompt, y_sample = _trunks(
        [x_prompt, x_sample], g_mix_e, w_in_e, lambda_qk, subln_g, conv_w, w_out_e, g_ffn_e, w_gate_up, w_down,
        g_mix_o, s5_a_re, s5_a_im, s5_log_step, s5_b_re, s5_b_im, s5_c_re, s5_c_im, s5_d,
        w_glu, g_ffn_o, w_router, w_exp_gate_up, w_exp_down, g_final)
    return (y_prompt, y_sample)
```

```python
import functools
import math

import jax
import jax.numpy as jnp
from jax import lax
from jax.experimental import pallas as pl
from jax.experimental.pallas import tpu as pltpu

F32 = jnp.float32
BF16 = jnp.bfloat16

D_MODEL = 2048
ATTN_WIDTH = D_MODEL // 2
CONV_CH = D_MODEL - ATTN_WIDTH
N_HEADS = 8
QK_DIM = 64
V_DIM = 2 * QK_DIM
ROPE_DIM = QK_DIM // 4
ROPE_THETA = 500000.0
IN_COLS = 3 * ATTN_WIDTH + 3 * CONV_CH
S5_P = 16
S5_GROUPS = D_MODEL // S5_P
S5_N = 64
FF_DENSE = 5632
N_EXPERTS = 8
FF_EXPERT = 7168
RMS_EPS = 1e-5
LAMBDA_INIT_0 = 0.8 - 0.6 * math.exp(-0.3 * 0)
QK_SCALE_LOG2 = QK_DIM ** -0.5 * math.log2(math.e)

LANES = 128
SUBLANES = 8
ATTN_SHORT_SEQ = 4096
ATTN_UNROLL = 5
S5_CHUNK = 128
S5_COLS = 512
S5_ROW = S5_CHUNK * S5_P
VMEM_LIMIT = 52 * 1024 * 1024


def _params(sem):
    return pltpu.CompilerParams(dimension_semantics=sem, vmem_limit_bytes=VMEM_LIMIT)


def _rms(x, g):
    ms = jnp.mean(x * x, axis=-1, keepdims=True)
    return x * lax.rsqrt(ms + RMS_EPS) * g


def _inproj_kernel(x_ref, g_ref, w_ref, ra_ref, rm_ref, rp_ref, o_ref, h_ref, *, tn, n_q, n_rope):
    j = pl.program_id(1)

    @pl.when(j == 0)
    def _():
        h_ref[...] = _rms(x_ref[...], g_ref[...]).astype(BF16)

    acc = jnp.dot(h_ref[...], w_ref[...], preferred_element_type=F32)

    @pl.when(j < n_rope)
    def _():
        scale = jnp.where(j < n_q, QK_SCALE_LOG2, 1.0).astype(F32)
        a = ra_ref[...] * scale
        bm = rm_ref[...] * scale
        bp = rp_ref[...] * scale
        for c in range(tn // LANES):
            t = acc[:, c * LANES:(c + 1) * LANES]
            r = t * a + pltpu.roll(t, LANES - ROPE_DIM // 2, 1) * bm + pltpu.roll(t, ROPE_DIM // 2, 1) * bp
            o_ref[:, c * LANES:(c + 1) * LANES] = r.astype(o_ref.dtype)

    @pl.when(j >= n_rope)
    def _():
        o_ref[...] = acc.astype(o_ref.dtype)


def _rope_tables(seq):
    half = ROPE_DIM // 2
    inv = 1.0 / (ROPE_THETA ** (jnp.arange(0, ROPE_DIM, 2, dtype=F32) / ROPE_DIM))
    ang = jnp.arange(seq, dtype=F32)[:, None] * inv[None, :]
    cos, sin = jnp.cos(ang), jnp.sin(ang)
    ones = jnp.ones((seq, QK_DIM - ROPE_DIM), F32)
    zeros = jnp.zeros((seq, QK_DIM - ROPE_DIM), F32)
    zh = jnp.zeros((seq, half), F32)
    a = jnp.concatenate([cos, cos, ones], axis=-1)
    bm = jnp.concatenate([-sin, zh, zeros], axis=-1)
    bp = jnp.concatenate([zh, sin, zeros], axis=-1)
    rep = LANES // QK_DIM
    return jnp.tile(a, (1, rep)), jnp.tile(bm, (1, rep)), jnp.tile(bp, (1, rep))


def _in_proj(x2, g, w_bf, seq):
    t, d = x2.shape
    tm = min(1024, seq)
    tn = 1024
    ra, rm, rp = _rope_tables(seq)
    nblk = seq // tm
    rope_spec = pl.BlockSpec((tm, LANES), lambda i, j: (i % nblk, 0))
    return pl.pallas_call(
        functools.partial(_inproj_kernel, tn=tn, n_q=ATTN_WIDTH // tn, n_rope=2 * ATTN_WIDTH // tn),
        grid=(t // tm, IN_COLS // tn),
        in_specs=[
            pl.BlockSpec((tm, d), lambda i, j: (i, 0)),
            pl.BlockSpec((1, d), lambda i, j: (0, 0)),
            pl.BlockSpec((d, tn), lambda i, j: (0, j)),
            rope_spec, rope_spec, rope_spec,
        ],
        out_specs=pl.BlockSpec((tm, tn), lambda i, j: (i, j)),
        out_shape=jax.ShapeDtypeStruct((t, IN_COLS), BF16),
        scratch_shapes=[pltpu.VMEM((tm, d), BF16)],
        compiler_params=_params(("parallel", "arbitrary")),
        name="in_proj",
    )(x2, g.reshape(1, d), w_bf, ra, rm, rp)


def _attn_kernel(q_ref, k_ref, v_ref, lq_ref, sg_ref, o_ref, qq_ref, vt_ref, s0_ref, s1_ref, acc_ref,
                 *, tq, tk, nk):
    nq = 2 * tq

    @pl.when(pl.program_id(2) == 0)
    def _():
        def flip(c, carry):
            start = pl.multiple_of(c * tk, tk)
            vt_ref[:, pl.ds(start, tk)] = v_ref[pl.ds(start, tk), :].astype(F32).T.astype(BF16)
            return carry
        lax.fori_loop(0, nk, flip, 0)

    qt = q_ref[...].astype(F32).T.astype(BF16)
    dim = lax.broadcasted_iota(jnp.int32, (V_DIM, tq), 0)
    zero = jnp.zeros_like(qt)
    qq_ref[:, 0:tq] = jnp.where(dim < QK_DIM, qt, zero)
    qq_ref[:, tq:nq] = jnp.where(dim >= QK_DIM, qt, zero)
    acc_ref[...] = jnp.zeros(acc_ref.shape, F32)

    def scores(j, s_ref):
        start = pl.multiple_of(j * tk, tk)
        s = jnp.dot(k_ref[pl.ds(start, tk), :], qq_ref[...], preferred_element_type=F32)
        s_ref[...] = s
        return jnp.max(s, axis=0, keepdims=True)

    def accumulate(j, s_ref, carry, block_max):
        m_prev, l_prev = carry
        m_new = jnp.maximum(m_prev, block_max)
        alpha = jnp.exp2(m_prev - m_new)
        p = jnp.exp2(s_ref[...] - m_new)
        l_new = alpha * l_prev + jnp.sum(p, axis=0, keepdims=True)
        start = pl.multiple_of(j * tk, tk)
        pv = jnp.dot(vt_ref[:, pl.ds(start, tk)], p.astype(BF16), preferred_element_type=F32)
        acc_ref[...] = alpha * acc_ref[...] + pv
        return m_new, l_new

    def body(jj, carry):
        ml, max_even = carry
        j = 2 * jj
        max_odd = scores(j + 1, s1_ref)
        ml = accumulate(j, s0_ref, ml, max_even)
        max_even = scores(j + 2, s0_ref)
        ml = accumulate(j + 1, s1_ref, ml, max_odd)
        return ml, max_even

    ml0 = (jnp.full((1, nq), -jnp.inf, F32), jnp.zeros((1, nq), F32))
    ml, max_even = lax.fori_loop(0, nk // 2 - 1, body, (ml0, scores(0, s0_ref)), unroll=ATTN_UNROLL)
    max_odd = scores(nk - 1, s1_ref)
    ml = accumulate(nk - 2, s0_ref, ml, max_even)
    _, denom = accumulate(nk - 1, s1_ref, ml, max_odd)

    lq = lq_ref[...]
    lam = (jnp.exp(jnp.sum(lq[0:1, :] * lq[1:2, :], axis=-1, keepdims=True))
           - jnp.exp(jnp.sum(lq[2:3, :] * lq[3:4, :], axis=-1, keepdims=True)) + LAMBDA_INIT_0)
    ot = acc_ref[...] / denom
    diff = ot[:, 0:tq] - lam * ot[:, tq:nq]
    ms = jnp.mean(diff * diff, axis=0, keepdims=True)
    y = diff * lax.rsqrt(ms + RMS_EPS) * sg_ref[...] * (1.0 - LAMBDA_INIT_0)
    o_ref[...] = y.T.astype(o_ref.dtype)


def _attention(z3, lambda_qk, subln_g):
    b, s, _ = z3.shape
    tq = min(1024 if s <= ATTN_SHORT_SEQ else 512, s)
    tk = min(512, s // 2)
    assert s % tq == 0 and s % (2 * tk) == 0, "key chunks are processed in pairs"
    kcol = ATTN_WIDTH // V_DIM
    return pl.pallas_call(
        functools.partial(_attn_kernel, tq=tq, tk=tk, nk=s // tk),
        grid=(b, N_HEADS, s // tq),
        in_specs=[
            pl.BlockSpec((None, tq, V_DIM), lambda bi, h, i: (bi, i, h)),
            pl.BlockSpec((None, s, V_DIM), lambda bi, h, i: (bi, 0, kcol + h)),
            pl.BlockSpec((None, s, V_DIM), lambda bi, h, i: (bi, 0, 2 * kcol + h)),
            pl.BlockSpec((4, QK_DIM), lambda bi, h, i: (0, 0)),
            pl.BlockSpec((V_DIM, 1), lambda bi, h, i: (0, 0)),
        ],
        out_specs=pl.BlockSpec((None, tq, V_DIM), lambda bi, h, i: (bi, i, h)),
        out_shape=jax.ShapeDtypeStruct((b, s, ATTN_WIDTH), BF16),
        scratch_shapes=[
            pltpu.VMEM((V_DIM, 2 * tq), BF16),
            pltpu.VMEM((V_DIM, s), BF16),
            pltpu.VMEM((tk, 2 * tq), F32),
            pltpu.VMEM((tk, 2 * tq), F32),
            pltpu.VMEM((V_DIM, 2 * tq), F32),
        ],
        compiler_params=_params(("parallel", "parallel", "arbitrary")),
        name="diff_attention",
    )(z3, z3, z3, lambda_qk.astype(F32), subln_g.reshape(V_DIM, 1).astype(F32))


def _outproj_kernel(a_ref, bg_ref, cg_ref, xc_ref, cgp_ref, xcp_ref, cgn_ref, xcn_ref, cw_ref, x_ref, w_ref,
                    o_ref, *, tm, nt):
    i = pl.program_id(1)
    u = cg_ref[...].astype(F32) * xc_ref[...].astype(F32)
    u_before = cgp_ref[SUBLANES - 1:SUBLANES, :].astype(F32) * xcp_ref[SUBLANES - 1:SUBLANES, :].astype(F32)
    u_after = cgn_ref[0:1, :].astype(F32) * xcn_ref[0:1, :].astype(F32)
    u_before = u_before * (i > 0).astype(F32)
    u_after = u_after * (i < nt - 1).astype(F32)
    row = lax.broadcasted_iota(jnp.int32, u.shape, 0)
    u_prev = jnp.where(row == 0, u_before, pltpu.roll(u, 1, 0))
    u_next = jnp.where(row == tm - 1, u_after, pltpu.roll(u, tm - 1, 0))
    cw = cw_ref[...]
    y = u_prev * cw[0:1, :] + u * cw[1:2, :] + u_next * cw[2:3, :]
    c = (bg_ref[...].astype(F32) * y).astype(BF16)
    acc = jnp.dot(a_ref[...], w_ref[0:ATTN_WIDTH, :], preferred_element_type=F32)
    acc = acc + jnp.dot(c, w_ref[ATTN_WIDTH:D_MODEL, :], preferred_element_type=F32)
    o_ref[...] = x_ref[...] + acc


def _out_proj(a3, z3, conv_w, x3, w_bf):
    b, s, _ = z3.shape
    tm = min(512, s)
    nt = s // tm
    rb = tm // SUBLANES
    last = s // SUBLANES - 1
    cb = 3 * ATTN_WIDTH // CONV_CH
    main = lambda col: pl.BlockSpec((None, tm, CONV_CH), lambda bi, i: (bi, i, col))
    prev = lambda col: pl.BlockSpec((None, SUBLANES, CONV_CH), lambda bi, i: (bi, jnp.maximum(i * rb - 1, 0), col))
    nxt = lambda col: pl.BlockSpec((None, SUBLANES, CONV_CH), lambda bi, i: (bi, jnp.minimum((i + 1) * rb, last), col))
    return pl.pallas_call(
        functools.partial(_outproj_kernel, tm=tm, nt=nt),
        grid=(b, nt),
        in_specs=[
            pl.BlockSpec((None, tm, ATTN_WIDTH), lambda bi, i: (bi, i, 0)),
            main(cb), main(cb + 1), main(cb + 2),
            prev(cb + 1), prev(cb + 2), nxt(cb + 1), nxt(cb + 2),
            pl.BlockSpec((3, CONV_CH), lambda bi, i: (0, 0)),
            pl.BlockSpec((None, tm, D_MODEL), lambda bi, i: (bi, i, 0)),
            pl.BlockSpec((D_MODEL, D_MODEL), lambda bi, i: (0, 0)),
        ],
        out_specs=pl.BlockSpec((None, tm, D_MODEL), lambda bi, i: (bi, i, 0)),
        out_shape=jax.ShapeDtypeStruct((b, s, D_MODEL), F32),
        compiler_params=_params(("parallel", "arbitrary")),
        name="out_proj_conv",
    )(a3, z3, z3, z3, z3, z3, z3, z3, conv_w.astype(F32), x3, w_bf)


def _ffn_kernel(x_ref, g_ref, wg_ref, wu_ref, wd_ref, o_ref, h_ref):
    f = pl.program_id(1)

    @pl.when(f == 0)
    def _():
        x = x_ref[...]
        h_ref[...] = _rms(x, g_ref[...]).astype(BF16)
        o_ref[...] = x

    h = h_ref[...]
    gate = jnp.dot(h, wg_ref[...], preferred_element_type=F32)
    up = jnp.dot(h, wu_ref[...], preferred_element_type=F32)
    act = (gate * jax.nn.sigmoid(gate) * up).astype(BF16)
    o_ref[...] += jnp.dot(act, wd_ref[...], preferred_element_type=F32)


def _ffn(x2, g, wgu_bf, wd_bf):
    t, d = x2.shape
    tm = min(512, t)
    tf = 512
    nf = FF_DENSE // tf
    return pl.pallas_call(
        _ffn_kernel,
        grid=(t // tm, nf),
        in_specs=[
            pl.BlockSpec((tm, d), lambda i, f: (i, 0)),
            pl.BlockSpec((1, d), lambda i, f: (0, 0)),
            pl.BlockSpec((d, tf), lambda i, f: (0, f)),
            pl.BlockSpec((d, tf), lambda i, f: (0, f + nf)),
            pl.BlockSpec((tf, d), lambda i, f: (f, 0)),
        ],
        out_specs=pl.BlockSpec((tm, d), lambda i, f: (i, 0)),
        out_shape=jax.ShapeDtypeStruct((t, d), F32),
        scratch_shapes=[pltpu.VMEM((tm, d), BF16)],
        compiler_params=_params(("parallel", "arbitrary")),
        name="ffn_swiglu",
    )(x2, g.reshape(1, d), wgu_bf, wgu_bf, wd_bf)


def _norm_kernel(x_ref, g_ref, o_ref):
    def rows(c, carry):
        r = pl.ds(pl.multiple_of(c * ROW_CHUNK, ROW_CHUNK), ROW_CHUNK)
        o_ref[:, r] = _rms(x_ref[r, :], g_ref[...]).T.astype(o_ref.dtype)
        return carry
    lax.fori_loop(0, x_ref.shape[0] // ROW_CHUNK, rows, 0)


def _norm_t(x2, g):
    t, d = x2.shape
    tm = min(512, t)
    return pl.pallas_call(
        _norm_kernel,
        grid=(t // tm,),
        in_specs=[pl.BlockSpec((tm, d), lambda i: (i, 0)), pl.BlockSpec((1, d), lambda i: (0, 0))],
        out_specs=pl.BlockSpec((d, tm), lambda i: (0, i)),
        out_shape=jax.ShapeDtypeStruct((d, t), BF16),
        compiler_params=_params(("parallel",)),
        name="s5_norm",
    )(x2, g.reshape(1, d))


_K_BP, _K_CP, _K_BW, _K_CS = 0, 1, 2, 3
_N_KIND = 4
_SCAN_STEPS_MAX = 16


def _s5_kernel(*refs, rows_per, n_steps):
    n_in = len(rows_per)
    u_refs = refs[:n_in]
    pt_ref, yt_ref, al_ref, pos_ref, rem_ref = refs[n_in:n_in + 5]
    o_refs = refs[n_in + 5:2 * n_in + 5]
    tab_ref, toep_ref = refs[2 * n_in + 5:]
    rows = sum(rows_per)

    for d in range(2):
        for kind in range(_N_KIND):
            x = pt_ref[d * _N_KIND + kind]
            xs = pltpu.roll(x, S5_N, 1)
            par = 0 if kind in (_K_BP, _K_BW) else 2
            for q in range(S5_P):
                ya = yt_ref[d * 4 + par, q:q + 1, :]
                yb = yt_ref[d * 4 + par + 1, q:q + 1, :]
                tab_ref[d * _N_KIND + kind, q * S5_CHUNK:(q + 1) * S5_CHUNK, :] = (x * ya + xs * yb).astype(BF16)

    nt = (((1,), (1,)), ((), ()))
    s_in = lax.broadcasted_iota(jnp.int32, (S5_ROW, S5_COLS), 0) & (S5_CHUNK - 1)
    t_out = lax.broadcasted_iota(jnp.int32, (S5_ROW, S5_COLS), 1) & (S5_CHUNK - 1)
    for c in range(S5_ROW // S5_COLS):
        cols = slice(c * S5_COLS, (c + 1) * S5_COLS)
        t_f = lax.dot_general(tab_ref[_K_BP], tab_ref[_K_CP, cols, :], nt, preferred_element_type=F32)
        t_b = lax.dot_general(tab_ref[_N_KIND + _K_BP], tab_ref[_N_KIND + _K_CP, cols, :], nt,
                              preferred_element_type=F32)
        toep_ref[:, cols] = (jnp.where(s_in <= t_out, t_f, 0.0) + jnp.where(s_in >= t_out, t_b, 0.0)).astype(BF16)

    u = jnp.concatenate([jnp.concatenate([u_ref[q] for u_ref in u_refs], axis=0) for q in range(S5_P)], axis=1)
    half = S5_ROW // 2
    y = jnp.concatenate([jnp.dot(u, toep_ref[:, 0:half], preferred_element_type=F32),
                         jnp.dot(u, toep_ref[:, half:S5_ROW], preferred_element_type=F32)], axis=1)

    for d in range(2):
        fwd = d == 0
        dist = pos_ref[...] if fwd else rem_ref[...]
        st = jnp.dot(u, tab_ref[d * _N_KIND + _K_BW], preferred_element_type=F32)
        for k in range(n_steps):
            step = 1 << k
            sh = pltpu.roll(st, step if fwd else rows - step, 0)
            sh = jnp.where(dist >= step, sh, 0.0)
            shs = pltpu.roll(sh, S5_N, 1)
            pa = al_ref[d * 2 * _SCAN_STEPS_MAX + k:d * 2 * _SCAN_STEPS_MAX + k + 1, :]
            pb = al_ref[d * 2 * _SCAN_STEPS_MAX + _SCAN_STEPS_MAX + k:d * 2 * _SCAN_STEPS_MAX + _SCAN_STEPS_MAX + k + 1, :]
            st = st + sh * pa + shs * pb
        inc = pltpu.roll(st, 1 if fwd else rows - 1, 0)
        inc = jnp.where(dist >= 1, inc, 0.0).astype(BF16)
        y = y + lax.dot_general(inc, tab_ref[d * _N_KIND + _K_CS], nt, preferred_element_type=F32)

    off = 0
    for o_ref, r in zip(o_refs, rows_per):
        for p in range(S5_P):
            o_ref[p] = y[off:off + r, p * S5_CHUNK:(p + 1) * S5_CHUNK].astype(o_ref.dtype)
        off += r


def _s5_tables(a_re, a_im, log_step, b_re, b_im, c_re, c_im, n_steps):
    dt = jnp.exp(log_step.astype(F32))[..., None]
    a_re = a_re.astype(F32)
    a_im = a_im.astype(F32)
    lr, li = a_re * dt, a_im * dt
    er = jnp.exp(lr)
    abr, abi = er * jnp.cos(li), er * jnp.sin(li)
    den = a_re * a_re + a_im * a_im
    fr = ((abr - 1.0) * a_re + abi * a_im) / den
    fi = (abi * a_re - (abr - 1.0) * a_im) / den
    b_re = b_re.astype(F32)
    b_im = b_im.astype(F32)
    bbr = fr[..., None] * b_re - fi[..., None] * b_im
    bbi = fr[..., None] * b_im + fi[..., None] * b_re
    bbr, bbi = jnp.swapaxes(bbr, -1, -2), jnp.swapaxes(bbi, -1, -2)
    cr, ci = c_re.astype(F32), c_im.astype(F32)
    cat = lambda lo, hi: jnp.concatenate([lo, hi], axis=-1)
    yt = jnp.stack([cat(bbr, bbr), cat(-bbi, bbi), cat(cr, -cr), cat(-ci, -ci)], axis=2)
    yt = jnp.swapaxes(yt, 0, 1).reshape(S5_GROUPS, 2 * 4, S5_P, LANES)

    idx = jnp.arange(S5_CHUNK, dtype=F32)
    length = float(S5_CHUNK)
    expo = jnp.stack([
        jnp.stack([-idx, idx, length - 1.0 - idx, idx + 1.0]),
        jnp.stack([idx, -idx, idx, length - idx]),
    ])
    e = expo[:, None, :, :, None]
    mag = jnp.exp(lr[:, :, None, None, :] * e)
    ang = li[:, :, None, None, :] * e
    pr, pi = mag * jnp.cos(ang), mag * jnp.sin(ang)
    pt = cat(pr, pi)
    pt = jnp.swapaxes(pt, 0, 1).reshape(S5_GROUPS, 2 * _N_KIND, S5_CHUNK, LANES)

    steps = length * (2.0 ** jnp.arange(_SCAN_STEPS_MAX, dtype=F32))
    se = steps[None, None, :, None]
    smag = jnp.exp(lr[:, :, None, :] * se)
    sang = li[:, :, None, :] * se
    sr, si = smag * jnp.cos(sang), smag * jnp.sin(sang)
    al = jnp.concatenate([cat(sr, sr), cat(-si, si)], axis=2)
    al = jnp.swapaxes(al, 0, 1).reshape(S5_GROUPS, 2 * 2 * _SCAN_STEPS_MAX, LANES)
    del n_steps
    return pt, yt, al


def _s5(ut_list, shapes, a_re, a_im, log_step, b_re, b_im, c_re, c_im):
    pos_parts, rem_parts, rows_per, max_cps = [], [], [], 1
    for (b, s) in shapes:
        cps = s // S5_CHUNK
        max_cps = max(max_cps, cps)
        p = jnp.tile(jnp.arange(cps, dtype=jnp.int32), b)
        pos_parts.append(p)
        rem_parts.append(cps - 1 - p)
        rows_per.append(b * cps)
    n_steps = max(1, (max_cps - 1).bit_length())
    rows = sum(rows_per)
    pos = jnp.broadcast_to(jnp.concatenate(pos_parts)[:, None], (rows, LANES))
    rem = jnp.broadcast_to(jnp.concatenate(rem_parts)[:, None], (rows, LANES))
    pt, yt, al = _s5_tables(a_re, a_im, log_step, b_re, b_im, c_re, c_im, n_steps)

    u4 = [ut.reshape(S5_GROUPS, S5_P, r, S5_CHUNK) for ut, r in zip(ut_list, rows_per)]
    io_specs = [pl.BlockSpec((None, S5_P, r, S5_CHUNK), lambda g: (g, 0, 0, 0)) for r in rows_per]
    outs = pl.pallas_call(
        functools.partial(_s5_kernel, rows_per=tuple(rows_per), n_steps=n_steps),
        grid=(S5_GROUPS,),
        in_specs=io_specs + [
            pl.BlockSpec((None, 2 * _N_KIND, S5_CHUNK, LANES), lambda g: (g, 0, 0, 0)),
            pl.BlockSpec((None, 2 * 4, S5_P, LANES), lambda g: (g, 0, 0, 0)),
            pl.BlockSpec((None, 2 * 2 * _SCAN_STEPS_MAX, LANES), lambda g: (g, 0, 0)),
            pl.BlockSpec((rows, LANES), lambda g: (0, 0)),
            pl.BlockSpec((rows, LANES), lambda g: (0, 0)),
        ],
        out_specs=io_specs,
        out_shape=[jax.ShapeDtypeStruct((S5_GROUPS, S5_P, r, S5_CHUNK), BF16) for r in rows_per],
        scratch_shapes=[pltpu.VMEM((2 * _N_KIND, S5_ROW, LANES), BF16), pltpu.VMEM((S5_ROW, S5_ROW), BF16)],
        compiler_params=_params(("parallel",)),
        name="s5_chunked",
    )(*u4, pt, yt, al, pos, rem)
    return [o.reshape(D_MODEL, r * S5_CHUNK) for o, r in zip(outs, rows_per)]


def _glu_kernel(x_ref, yt_ref, g_ref, d_ref, wa_ref, wb_ref, o_ref, a_ref, *, tn):
    j = pl.program_id(1)

    @pl.when(j == 0)
    def _():
        def rows(c, carry):
            r = pl.ds(pl.multiple_of(c * ROW_CHUNK, ROW_CHUNK), ROW_CHUNK)
            h = _rms(x_ref[r, :], g_ref[...])
            t = (yt_ref[:, r].astype(F32).T + d_ref[...] * h)
            gelu = 0.5 * t * (1.0 + jnp.tanh(math.sqrt(2.0 / math.pi) * (t + 0.044715 * (t * t * t))))
            a_ref[r, :] = gelu.astype(BF16)
            return carry
        lax.fori_loop(0, a_ref.shape[0] // ROW_CHUNK, rows, 0)

    a = a_ref[...]
    ga = jnp.dot(a, wa_ref[...], preferred_element_type=F32)
    gb = jnp.dot(a, wb_ref[...], preferred_element_type=F32)
    xs = x_ref[:, pl.ds(pl.multiple_of(j * tn, tn), tn)]
    o_ref[...] = xs + ga * jax.nn.sigmoid(gb)


def _glu(x2, yt, g, dskip, w_bf):
    t, d = x2.shape
    tm = min(1024, t)
    tn = 512
    nj = d // tn
    return pl.pallas_call(
        functools.partial(_glu_kernel, tn=tn),
        grid=(t // tm, nj),
        in_specs=[
            pl.BlockSpec((tm, d), lambda i, j: (i, 0)),
            pl.BlockSpec((d, tm), lambda i, j: (0, i)),
            pl.BlockSpec((1, d), lambda i, j: (0, 0)),
            pl.BlockSpec((1, d), lambda i, j: (0, 0)),
            pl.BlockSpec((d, tn), lambda i, j: (0, j)),
            pl.BlockSpec((d, tn), lambda i, j: (0, j + nj)),
        ],
        out_specs=pl.BlockSpec((tm, tn), lambda i, j: (i, j)),
        out_shape=jax.ShapeDtypeStruct((t, d), F32),
        scratch_shapes=[pltpu.VMEM((tm, d), BF16)],
        compiler_params=_params(("parallel", "arbitrary")),
        name="s5_glu",
    )(x2, yt, g.reshape(1, d), dskip.reshape(1, d).astype(F32), w_bf, w_bf)


MOE_TILE = 1024
MOE_FF_TILE = 512
COMBINE_TILE = 256
DMA_UNROLL = 8
ROW_CHUNK = 128


def _route_kernel(x_ref, g_ref, wr_ref, idx_ref, gate_ref):
    h = _rms(x_ref[...], g_ref[...])
    w = wr_ref[...]
    h_hi, w_hi = h.astype(BF16), w.astype(BF16)
    h_lo = (h - h_hi.astype(F32)).astype(BF16)
    w_lo = (w - w_hi.astype(F32)).astype(BF16)
    logits = (jnp.dot(h_hi, w_hi, preferred_element_type=F32) + jnp.dot(h_lo, w_hi, preferred_element_type=F32)
              + jnp.dot(h_hi, w_lo, preferred_element_type=F32))
    lane = lax.broadcasted_iota(jnp.int32, logits.shape, 1)
    neg = jnp.float32(-jnp.inf)
    lg = jnp.where(lane < N_EXPERTS, logits, neg)
    m1 = jnp.max(lg, axis=-1, keepdims=True)
    i1 = jnp.min(jnp.where(lg == m1, lane, LANES), axis=-1, keepdims=True)
    lg2 = jnp.where(lane == i1, neg, lg)
    m2 = jnp.max(lg2, axis=-1, keepdims=True)
    i2 = jnp.min(jnp.where(lg2 == m2, lane, LANES), axis=-1, keepdims=True)
    ex = jnp.exp(m2 - m1)
    den = 1.0 + ex
    idx_ref[...] = jnp.where(lane == 0, i1, jnp.where(lane == 1, i2, 0))
    gate_ref[...] = jnp.where(lane == 0, 1.0 / den, jnp.where(lane == 1, ex / den, 0.0))


def _route(x2, g, w_router):
    t, d = x2.shape
    tm = min(512, t)
    wr = jnp.zeros((d, LANES), F32).at[:, :N_EXPERTS].set(w_router.astype(F32))
    return pl.pallas_call(
        _route_kernel,
        grid=(t // tm,),
        in_specs=[
            pl.BlockSpec((tm, d), lambda i: (i, 0)),
            pl.BlockSpec((1, d), lambda i: (0, 0)),
            pl.BlockSpec((d, LANES), lambda i: (0, 0)),
        ],
        out_specs=[pl.BlockSpec((tm, LANES), lambda i: (i, 0)), pl.BlockSpec((tm, LANES), lambda i: (i, 0))],
        out_shape=[jax.ShapeDtypeStruct((t, LANES), jnp.int32), jax.ShapeDtypeStruct((t, LANES), F32)],
        compiler_params=_params(("parallel",)),
        name="moe_route",
    )(x2, g.reshape(1, d), wr)


def _expert_kernel(src_ref, te_ref, tv_ref, x_hbm, g_ref, wg_ref, wu_ref, wd_ref, o_ref, xbuf, hb_ref, sem,
                   *, tm, n_tiles):
    i = pl.program_id(0)
    f = pl.program_id(1)

    def row_copy(tile, r):
        tok = src_ref[tile * tm + r]
        return pltpu.make_async_copy(x_hbm.at[pl.ds(tok, 1)], xbuf.at[pl.ds(r, 1)], sem.at[0])

    def start_gather(tile):
        def go(r, c):
            row_copy(tile, r).start()
            return c
        lax.fori_loop(0, tm, go, 0, unroll=DMA_UNROLL)

    def wait_gather(tile):
        def go(r, c):
            row_copy(tile, r).wait()
            return c
        lax.fori_loop(0, tm, go, 0, unroll=DMA_UNROLL)

    @pl.when(f == 0)
    def _():
        @pl.when(i == 0)
        def _():
            start_gather(0)

        wait_gather(i)
        chunk = min(ROW_CHUNK, tm)

        def rows(c, carry):
            r = pl.ds(pl.multiple_of(c * chunk, chunk), chunk)
            hb_ref[r, :] = _rms(xbuf[r, :], g_ref[...]).astype(BF16)
            return carry
        lax.fori_loop(0, tm // chunk, rows, 0)

        @pl.when(i + 1 < n_tiles)
        def _():
            start_gather(i + 1)

        o_ref[...] = jnp.zeros(o_ref.shape, F32)

    def swiglu(rows):
        h = hb_ref[0:rows, :]
        gate = jnp.dot(h, wg_ref[...], preferred_element_type=F32)
        up = jnp.dot(h, wu_ref[...], preferred_element_type=F32)
        act = (gate * jax.nn.sigmoid(gate) * up).astype(BF16)
        o_ref[0:rows, :] += jnp.dot(act, wd_ref[...], preferred_element_type=F32)

    n_valid = tv_ref[i]

    @pl.when(n_valid > tm // 2)
    def _():
        swiglu(tm)

    @pl.when((n_valid > 0) & (n_valid <= tm // 2))
    def _():
        swiglu(tm // 2)


def _combine_kernel(dest_ref, x_ref, gate_ref, gf_ref, y_hbm, o_ref, ybuf, sem, *, tm, n_tiles):
    i = pl.program_id(0)
    slot = i % 2

    def row_copy(tile, r, k, buf):
        row = dest_ref[(tile * tm + r) * 2 + k]
        return pltpu.make_async_copy(y_hbm.at[pl.ds(row, 1)], ybuf.at[buf, k, pl.ds(r, 1)], sem.at[buf])

    def start_gather(tile, buf):
        def go(r, c):
            row_copy(tile, r, 0, buf).start()
            row_copy(tile, r, 1, buf).start()
            return c
        lax.fori_loop(0, tm, go, 0, unroll=DMA_UNROLL)

    def wait_gather(tile, buf):
        def go(r, c):
            row_copy(tile, r, 0, buf).wait()
            row_copy(tile, r, 1, buf).wait()
            return c
        lax.fori_loop(0, tm, go, 0, unroll=DMA_UNROLL)

    @pl.when(i == 0)
    def _():
        start_gather(0, 0)

    @pl.when(i + 1 < n_tiles)
    def _():
        start_gather(i + 1, 1 - slot)

    wait_gather(i, slot)
    gate = gate_ref[...]
    y = x_ref[...] + gate[:, 0:1] * ybuf[slot, 0] + gate[:, 1:2] * ybuf[slot, 1]
    o_ref[...] = _rms(y, gf_ref[...])


def _moe(x2, g, w_router, wgu_bf, wd_bf, g_final):
    t, d = x2.shape
    tm = MOE_TILE if t >= 8 * MOE_TILE else 64
    tf = MOE_FF_TILE
    nf = FF_EXPERT // tf
    ridx, rgate = _route(x2, g, w_router)

    n_assign = 2 * t
    expert = ridx[:, 0:2].reshape(n_assign)
    onehot = (expert[:, None] == jnp.arange(N_EXPERTS, dtype=jnp.int32)[None, :]).astype(jnp.int32)
    csum = jnp.cumsum(onehot, axis=0)
    rank = jnp.sum((csum - onehot) * onehot, axis=1)
    padded = ((csum[-1] + tm - 1) // tm) * tm
    pend = jnp.cumsum(padded)
    dest = jnp.sum(onehot * (pend - padded)[None, :], axis=1) + rank
    n_slots = n_assign + N_EXPERTS * tm
    n_tiles = n_slots // tm
    src = jnp.zeros((n_slots,), jnp.int32).at[dest].set(jnp.arange(n_assign, dtype=jnp.int32) // 2)
    tile_start = jnp.arange(n_tiles, dtype=jnp.int32) * tm
    tile_expert = jnp.minimum(jnp.sum((tile_start[:, None] >= pend[None, :]).astype(jnp.int32), axis=1),
                              N_EXPERTS - 1)
    counts = csum[-1]
    group_end = (pend - padded + counts)[tile_expert]
    tile_valid = jnp.where(tile_start < pend[-1], jnp.clip(group_end - tile_start, 0, tm), 0).astype(jnp.int32)

    wcol = lambda i, f, tv: jnp.where(tv[i] != 0, f, nf - 1)
    ys = pl.pallas_call(
        functools.partial(_expert_kernel, tm=tm, n_tiles=n_tiles),
        grid_spec=pltpu.PrefetchScalarGridSpec(
            num_scalar_prefetch=3,
            grid=(n_tiles, nf),
            in_specs=[
                pl.BlockSpec(memory_space=pl.ANY),
                pl.BlockSpec((1, d), lambda i, f, src, te, tv: (0, 0)),
                pl.BlockSpec((None, d, tf), lambda i, f, src, te, tv: (te[i], 0, wcol(i, f, tv))),
                pl.BlockSpec((None, d, tf), lambda i, f, src, te, tv: (te[i], 0, wcol(i, f, tv) + nf)),
                pl.BlockSpec((None, tf, d), lambda i, f, src, te, tv: (te[i], wcol(i, f, tv), 0)),
            ],
            out_specs=pl.BlockSpec((tm, d), lambda i, f, src, te, tv: (i, 0)),
            scratch_shapes=[
                pltpu.VMEM((tm, d), F32),
                pltpu.VMEM((tm, d), BF16),
                pltpu.SemaphoreType.DMA((1,)),
            ],
        ),
        out_shape=jax.ShapeDtypeStruct((n_slots, d), F32),
        compiler_params=_params(("arbitrary", "arbitrary")),
        name="moe_experts",
    )(src, tile_expert, tile_valid, x2, g.reshape(1, d), wgu_bf, wgu_bf, wd_bf)

    tc = min(COMBINE_TILE, t)
    nc = t // tc
    return pl.pallas_call(
        functools.partial(_combine_kernel, tm=tc, n_tiles=nc),
        grid_spec=pltpu.PrefetchScalarGridSpec(
            num_scalar_prefetch=1,
            grid=(nc,),
            in_specs=[
                pl.BlockSpec((tc, d), lambda i, dst: (i, 0)),
                pl.BlockSpec((tc, LANES), lambda i, dst: (i, 0)),
                pl.BlockSpec((1, d), lambda i, dst: (0, 0)),
                pl.BlockSpec(memory_space=pl.ANY),
            ],
            out_specs=pl.BlockSpec((tc, d), lambda i, dst: (i, 0)),
            scratch_shapes=[
                pltpu.VMEM((2, 2, tc, d), F32),
                pltpu.SemaphoreType.DMA((2,)),
            ],
        ),
        out_shape=jax.ShapeDtypeStruct((t, d), F32),
        compiler_params=_params(("arbitrary",)),
        name="moe_combine",
    )(dest, x2, rgate, g_final.reshape(1, d), ys)


def _trunks(xs, g_mix_e, w_in_e, lambda_qk, subln_g, conv_w, w_out_e, g_ffn_e, w_gate_up, w_down,
            g_mix_o, s5_a_re, s5_a_im, s5_log_step, s5_b_re, s5_b_im, s5_c_re, s5_c_im, s5_d,
            w_glu, g_ffn_o, w_router, w_exp_gate_up, w_exp_down, g_final):
    w_in = w_in_e[0].astype(BF16)
    w_out = w_out_e[0].astype(BF16)
    w_gu = w_gate_up[0].astype(BF16)
    w_dn = w_down[0].astype(BF16)
    w_gl = w_glu[0].astype(BF16)
    w_egu = w_exp_gate_up[0].astype(BF16)
    w_edn = w_exp_down[0].astype(BF16)

    shapes = [(x.shape[0], x.shape[1]) for x in xs]
    x1s, us = [], []
    for x in xs:
        b, s, d = x.shape
        z = _in_proj(x.reshape(b * s, d), g_mix_e[0], w_in, s).reshape(b, s, IN_COLS)
        a = _attention(z, lambda_qk[0], subln_g[0])
        x1 = _out_proj(a, z, conv_w[0], x, w_out).reshape(b * s, d)
        x1 = _ffn(x1, g_ffn_e[0], w_gu, w_dn)
        x1s.append(x1)
        us.append(_norm_t(x1, g_mix_o[0]))
    ys = _s5(us, shapes, s5_a_re[0], s5_a_im[0], s5_log_step[0], s5_b_re[0], s5_b_im[0], s5_c_re[0], s5_c_im[0])
    outs = []
    for x1, y, (b, s) in zip(x1s, ys, shapes):
        x2 = _glu(x1, y, g_mix_o[0], s5_d[0], w_gl)
        out = _moe(x2, g_ffn_o[0], w_router[0], w_egu, w_edn, g_final)
        outs.append(out.reshape(b, s, D_MODEL))
    return outs


def kernel(x_prompt, x_sample, g_mix_e, w_in_e, lambda_qk, subln_g, conv_w, w_out_e, g_ffn_e, w_gate_up, w_down, g_mix_o, s5_a_re, s5_a_im, s5_log_step, s5_b_re, s5_b_im, s5_c_re, s5_c_im, s5_d, w_glu, g_ffn_o, w_router, w_exp_gate_up, w_exp_down, g_final):
    y_prompt, y_sample = _trunks(
        [x_prompt, x_sample], g_mix_e, w_in_e, lambda_qk, subln_g, conv_w, w_out_e, g_ffn_e, w_gate_up, w_down,
        g_mix_o, s5_a_re, s5_a_im, s5_log_step, s5_b_re, s5_b_im, s5_c_re, s5_c_im, s5_d,
        w_glu, g_ffn_o, w_router, w_exp_gate_up, w_exp_down, g_final)
    return (y_prompt, y_sample)
```

```python
import functools
import math

import jax
import jax.numpy as jnp
from jax import lax
from jax.experimental import pallas as pl
from jax.experimental.pallas import tpu as pltpu

F32 = jnp.float32
BF16 = jnp.bfloat16

D_MODEL = 2048
ATTN_WIDTH = D_MODEL // 2
CONV_CH = D_MODEL - ATTN_WIDTH
N_HEADS = 8
QK_DIM = 64
V_DIM = 2 * QK_DIM
ROPE_DIM = QK_DIM // 4
ROPE_THETA = 500000.0
IN_COLS = 3 * ATTN_WIDTH + 3 * CONV_CH
S5_P = 16
S5_GROUPS = D_MODEL // S5_P
S5_N = 64
FF_DENSE = 5632
N_EXPERTS = 8
FF_EXPERT = 7168
RMS_EPS = 1e-5
LAMBDA_INIT_0 = 0.8 - 0.6 * math.exp(-0.3 * 0)
QK_SCALE_LOG2 = QK_DIM ** -0.5 * math.log2(math.e)

LANES = 128
SUBLANES = 8
ROW_CHUNK = 128
ATTN_SHORT_SEQ = 4096
ATTN_UNROLL = 5
MOE_TILE = 1024
MOE_FF_TILE = 512
COMBINE_TILE = 256
DMA_UNROLL = 8
S5_CHUNK = 128
S5_COLS = 512
S5_ROW = S5_CHUNK * S5_P
VMEM_LIMIT = 52 * 1024 * 1024


def _params(sem):
    return pltpu.CompilerParams(dimension_semantics=sem, vmem_limit_bytes=VMEM_LIMIT)


def _rms(x, g):
    ms = jnp.mean(x * x, axis=-1, keepdims=True)
    return x * lax.rsqrt(ms + RMS_EPS) * g


def _inproj_kernel(x_ref, g_ref, w_ref, ra_ref, rm_ref, rp_ref, o_ref, h_ref, *, tn, n_q, n_rope):
    j = pl.program_id(1)

    @pl.when(j == 0)
    def _():
        h_ref[...] = _rms(x_ref[...], g_ref[...]).astype(BF16)

    acc = jnp.dot(h_ref[...], w_ref[...], preferred_element_type=F32)

    @pl.when(j < n_rope)
    def _():
        scale = jnp.where(j < n_q, QK_SCALE_LOG2, 1.0).astype(F32)
        a = ra_ref[...] * scale
        bm = rm_ref[...] * scale
        bp = rp_ref[...] * scale
        for c in range(tn // LANES):
            t = acc[:, c * LANES:(c + 1) * LANES]
            tb = t.astype(BF16)
            r = (t * a + pltpu.roll(tb, LANES - ROPE_DIM // 2, 1).astype(F32) * bm
                 + pltpu.roll(tb, ROPE_DIM // 2, 1).astype(F32) * bp)
            o_ref[:, c * LANES:(c + 1) * LANES] = r.astype(o_ref.dtype)

    @pl.when(j >= n_rope)
    def _():
        o_ref[...] = acc.astype(o_ref.dtype)


def _rope_tables(seq):
    half = ROPE_DIM // 2
    inv = 1.0 / (ROPE_THETA ** (jnp.arange(0, ROPE_DIM, 2, dtype=F32) / ROPE_DIM))
    ang = jnp.arange(seq, dtype=F32)[:, None] * inv[None, :]
    cos, sin = jnp.cos(ang), jnp.sin(ang)
    ones = jnp.ones((seq, QK_DIM - ROPE_DIM), F32)
    zeros = jnp.zeros((seq, QK_DIM - ROPE_DIM), F32)
    zh = jnp.zeros((seq, half), F32)
    a = jnp.concatenate([cos, cos, ones], axis=-1)
    bm = jnp.concatenate([-sin, zh, zeros], axis=-1)
    bp = jnp.concatenate([zh, sin, zeros], axis=-1)
    rep = LANES // QK_DIM
    return jnp.tile(a, (1, rep)), jnp.tile(bm, (1, rep)), jnp.tile(bp, (1, rep))


def _in_proj(x2, g, w_bf, seq):
    t, d = x2.shape
    tm = min(1024, seq)
    tn = 1024
    ra, rm, rp = _rope_tables(seq)
    nblk = seq // tm
    rope_spec = pl.BlockSpec((tm, LANES), lambda i, j: (i % nblk, 0))
    return pl.pallas_call(
        functools.partial(_inproj_kernel, tn=tn, n_q=ATTN_WIDTH // tn, n_rope=2 * ATTN_WIDTH // tn),
        grid=(t // tm, IN_COLS // tn),
        in_specs=[
            pl.BlockSpec((tm, d), lambda i, j: (i, 0)),
            pl.BlockSpec((1, d), lambda i, j: (0, 0)),
            pl.BlockSpec((d, tn), lambda i, j: (0, j)),
            rope_spec, rope_spec, rope_spec,
        ],
        out_specs=pl.BlockSpec((tm, tn), lambda i, j: (i, j)),
        out_shape=jax.ShapeDtypeStruct((t, IN_COLS), BF16),
        scratch_shapes=[pltpu.VMEM((tm, d), BF16)],
        compiler_params=_params(("parallel", "arbitrary")),
        name="in_proj",
    )(x2, g.reshape(1, d), w_bf, ra, rm, rp)


def _attn_kernel(q_ref, k_ref, v_ref, lq_ref, sg_ref, o_ref, qq_ref, vt_ref, s0_ref, s1_ref, acc_ref,
                 *, tq, tk, nk):
    nq = 2 * tq

    @pl.when(pl.program_id(2) == 0)
    def _():
        def flip(c, carry):
            start = pl.multiple_of(c * tk, tk)
            vt_ref[:, pl.ds(start, tk)] = v_ref[pl.ds(start, tk), :].astype(F32).T.astype(BF16)
            return carry
        lax.fori_loop(0, nk, flip, 0)

    qt = q_ref[...].astype(F32).T.astype(BF16)
    dim = lax.broadcasted_iota(jnp.int32, (V_DIM, tq), 0)
    zero = jnp.zeros_like(qt)
    qq_ref[:, 0:tq] = jnp.where(dim < QK_DIM, qt, zero)
    qq_ref[:, tq:nq] = jnp.where(dim >= QK_DIM, qt, zero)
    acc_ref[...] = jnp.zeros(acc_ref.shape, F32)

    def scores(j, s_ref):
        start = pl.multiple_of(j * tk, tk)
        s = jnp.dot(k_ref[pl.ds(start, tk), :], qq_ref[...], preferred_element_type=F32)
        s_ref[...] = s
        return jnp.max(s, axis=0, keepdims=True)

    def accumulate(j, s_ref, carry, block_max):
        m_prev, l_prev = carry
        m_new = jnp.maximum(m_prev, block_max)
        alpha = jnp.exp2(m_prev - m_new)
        p = jnp.exp2(s_ref[...] - m_new)
        l_new = alpha * l_prev + jnp.sum(p, axis=0, keepdims=True)
        start = pl.multiple_of(j * tk, tk)
        pv = jnp.dot(vt_ref[:, pl.ds(start, tk)], p.astype(BF16), preferred_element_type=F32)
        acc_ref[...] = alpha * acc_ref[...] + pv
        return m_new, l_new

    def body(jj, carry):
        ml, max_even = carry
        j = 2 * jj
        max_odd = scores(j + 1, s1_ref)
        ml = accumulate(j, s0_ref, ml, max_even)
        max_even = scores(j + 2, s0_ref)
        ml = accumulate(j + 1, s1_ref, ml, max_odd)
        return ml, max_even

    ml0 = (jnp.full((1, nq), -jnp.inf, F32), jnp.zeros((1, nq), F32))
    ml, max_even = lax.fori_loop(0, nk // 2 - 1, body, (ml0, scores(0, s0_ref)), unroll=ATTN_UNROLL)
    max_odd = scores(nk - 1, s1_ref)
    ml = accumulate(nk - 2, s0_ref, ml, max_even)
    _, denom = accumulate(nk - 1, s1_ref, ml, max_odd)

    lq = lq_ref[...]
    lam = (jnp.exp(jnp.sum(lq[0:1, :] * lq[1:2, :], axis=-1, keepdims=True))
           - jnp.exp(jnp.sum(lq[2:3, :] * lq[3:4, :], axis=-1, keepdims=True)) + LAMBDA_INIT_0)
    ot = acc_ref[...] / denom
    diff = ot[:, 0:tq] - lam * ot[:, tq:nq]
    ms = jnp.mean(diff * diff, axis=0, keepdims=True)
    y = diff * lax.rsqrt(ms + RMS_EPS) * sg_ref[...] * (1.0 - LAMBDA_INIT_0)
    o_ref[...] = y.T.astype(o_ref.dtype)


def _attention(z3, lambda_qk, subln_g):
    b, s, _ = z3.shape
    tq = min(1024 if s <= ATTN_SHORT_SEQ else 512, s)
    tk = min(512, s // 2)
    assert s % tq == 0 and s % (2 * tk) == 0, "key chunks are processed in pairs"
    kcol = ATTN_WIDTH // V_DIM
    return pl.pallas_call(
        functools.partial(_attn_kernel, tq=tq, tk=tk, nk=s // tk),
        grid=(b, N_HEADS, s // tq),
        in_specs=[
            pl.BlockSpec((None, tq, V_DIM), lambda bi, h, i: (bi, i, h)),
            pl.BlockSpec((None, s, V_DIM), lambda bi, h, i: (bi, 0, kcol + h)),
            pl.BlockSpec((None, s, V_DIM), lambda bi, h, i: (bi, 0, 2 * kcol + h)),
            pl.BlockSpec((4, QK_DIM), lambda bi, h, i: (0, 0)),
            pl.BlockSpec((V_DIM, 1), lambda bi, h, i: (0, 0)),
        ],
        out_specs=pl.BlockSpec((None, tq, V_DIM), lambda bi, h, i: (bi, i, h)),
        out_shape=jax.ShapeDtypeStruct((b, s, ATTN_WIDTH), BF16),
        scratch_shapes=[
            pltpu.VMEM((V_DIM, 2 * tq), BF16),
            pltpu.VMEM((V_DIM, s), BF16),
            pltpu.VMEM((tk, 2 * tq), F32),
            pltpu.VMEM((tk, 2 * tq), F32),
            pltpu.VMEM((V_DIM, 2 * tq), F32),
        ],
        compiler_params=_params(("parallel", "parallel", "arbitrary")),
        name="diff_attention",
    )(z3, z3, z3, lambda_qk.astype(F32), subln_g.reshape(V_DIM, 1).astype(F32))


def _outproj_kernel(a_ref, bg_ref, cg_ref, xc_ref, cgp_ref, xcp_ref, cgn_ref, xcn_ref, cw_ref, x_ref, w_ref,
                    o_ref, *, tm, nt):
    i = pl.program_id(1)
    u = cg_ref[...].astype(F32) * xc_ref[...].astype(F32)
    u_before = cgp_ref[SUBLANES - 1:SUBLANES, :].astype(F32) * xcp_ref[SUBLANES - 1:SUBLANES, :].astype(F32)
    u_after = cgn_ref[0:1, :].astype(F32) * xcn_ref[0:1, :].astype(F32)
    u_before = u_before * (i > 0).astype(F32)
    u_after = u_after * (i < nt - 1).astype(F32)
    row = lax.broadcasted_iota(jnp.int32, u.shape, 0)
    u_prev = jnp.where(row == 0, u_before, pltpu.roll(u, 1, 0))
    u_next = jnp.where(row == tm - 1, u_after, pltpu.roll(u, tm - 1, 0))
    cw = cw_ref[...]
    y = u_prev * cw[0:1, :] + u * cw[1:2, :] + u_next * cw[2:3, :]
    c = (bg_ref[...].astype(F32) * y).astype(BF16)
    acc = jnp.dot(a_ref[...], w_ref[0:ATTN_WIDTH, :], preferred_element_type=F32)
    acc = acc + jnp.dot(c, w_ref[ATTN_WIDTH:D_MODEL, :], preferred_element_type=F32)
    o_ref[...] = x_ref[...] + acc


def _out_proj(a3, z3, conv_w, x3, w_bf):
    b, s, _ = z3.shape
    tm = min(512, s)
    nt = s // tm
    rb = tm // SUBLANES
    last = s // SUBLANES - 1
    cb = 3 * ATTN_WIDTH // CONV_CH
    main = lambda col: pl.BlockSpec((None, tm, CONV_CH), lambda bi, i: (bi, i, col))
    prev = lambda col: pl.BlockSpec((None, SUBLANES, CONV_CH), lambda bi, i: (bi, jnp.maximum(i * rb - 1, 0), col))
    nxt = lambda col: pl.BlockSpec((None, SUBLANES, CONV_CH), lambda bi, i: (bi, jnp.minimum((i + 1) * rb, last), col))
    return pl.pallas_call(
        functools.partial(_outproj_kernel, tm=tm, nt=nt),
        grid=(b, nt),
        in_specs=[
            pl.BlockSpec((None, tm, ATTN_WIDTH), lambda bi, i: (bi, i, 0)),
            main(cb), main(cb + 1), main(cb + 2),
            prev(cb + 1), prev(cb + 2), nxt(cb + 1), nxt(cb + 2),
            pl.BlockSpec((3, CONV_CH), lambda bi, i: (0, 0)),
            pl.BlockSpec((None, tm, D_MODEL), lambda bi, i: (bi, i, 0)),
            pl.BlockSpec((D_MODEL, D_MODEL), lambda bi, i: (0, 0)),
        ],
        out_specs=pl.BlockSpec((None, tm, D_MODEL), lambda bi, i: (bi, i, 0)),
        out_shape=jax.ShapeDtypeStruct((b, s, D_MODEL), F32),
        compiler_params=_params(("parallel", "arbitrary")),
        name="out_proj_conv",
    )(a3, z3, z3, z3, z3, z3, z3, z3, conv_w.astype(F32), x3, w_bf)


def _ffn_kernel(x_ref, g_ref, wg_ref, wu_ref, wd_ref, o_ref, h_ref):
    f = pl.program_id(1)

    @pl.when(f == 0)
    def _():
        x = x_ref[...]
        h_ref[...] = _rms(x, g_ref[...]).astype(BF16)
        o_ref[...] = x

    h = h_ref[...]
    gate = jnp.dot(h, wg_ref[...], preferred_element_type=F32)
    up = jnp.dot(h, wu_ref[...], preferred_element_type=F32)
    act = (gate * jax.nn.sigmoid(gate) * up).astype(BF16)
    o_ref[...] += jnp.dot(act, wd_ref[...], preferred_element_type=F32)


def _ffn(x2, g, wgu_bf, wd_bf):
    t, d = x2.shape
    tm = min(512, t)
    tf = 512
    nf = FF_DENSE // tf
    return pl.pallas_call(
        _ffn_kernel,
        grid=(t // tm, nf),
        in_specs=[
            pl.BlockSpec((tm, d), lambda i, f: (i, 0)),
            pl.BlockSpec((1, d), lambda i, f: (0, 0)),
            pl.BlockSpec((d, tf), lambda i, f: (0, f)),
            pl.BlockSpec((d, tf), lambda i, f: (0, f + nf)),
            pl.BlockSpec((tf, d), lambda i, f: (f, 0)),
        ],
        out_specs=pl.BlockSpec((tm, d), lambda i, f: (i, 0)),
        out_shape=jax.ShapeDtypeStruct((t, d), F32),
        scratch_shapes=[pltpu.VMEM((tm, d), BF16)],
        compiler_params=_params(("parallel", "arbitrary")),
        name="ffn_swiglu",
    )(x2, g.reshape(1, d), wgu_bf, wgu_bf, wd_bf)


def _norm_kernel(x_ref, g_ref, o_ref):
    def rows(c, carry):
        r = pl.ds(pl.multiple_of(c * ROW_CHUNK, ROW_CHUNK), ROW_CHUNK)
        o_ref[:, r] = _rms(x_ref[r, :], g_ref[...]).T.astype(o_ref.dtype)
        return carry
    lax.fori_loop(0, x_ref.shape[0] // ROW_CHUNK, rows, 0)


def _norm_t(x2, g):
    t, d = x2.shape
    tm = min(512, t)
    return pl.pallas_call(
        _norm_kernel,
        grid=(t // tm,),
        in_specs=[pl.BlockSpec((tm, d), lambda i: (i, 0)), pl.BlockSpec((1, d), lambda i: (0, 0))],
        out_specs=pl.BlockSpec((d, tm), lambda i: (0, i)),
        out_shape=jax.ShapeDtypeStruct((d, t), BF16),
        compiler_params=_params(("parallel",)),
        name="s5_norm",
    )(x2, g.reshape(1, d))


_K_BP, _K_CP, _K_BW, _K_CS = 0, 1, 2, 3
_N_KIND = 4
_SCAN_STEPS_MAX = 16


def _s5_kernel(*refs, rows_per, n_steps):
    n_in = len(rows_per)
    u_refs = refs[:n_in]
    pt_ref, yt_ref, al_ref, dsk_ref, pos_ref, rem_ref = refs[n_in:n_in + 6]
    o_refs = refs[n_in + 6:2 * n_in + 6]
    tab_ref, toep_ref = refs[2 * n_in + 6:]
    rows = sum(rows_per)

    for d in range(2):
        for kind in range(_N_KIND):
            x = pt_ref[d * _N_KIND + kind]
            xs = pltpu.roll(x, S5_N, 1)
            par = 0 if kind in (_K_BP, _K_BW) else 2
            for q in range(S5_P):
                ya = yt_ref[d * 4 + par, q:q + 1, :]
                yb = yt_ref[d * 4 + par + 1, q:q + 1, :]
                tab_ref[d * _N_KIND + kind, q * S5_CHUNK:(q + 1) * S5_CHUNK, :] = (x * ya + xs * yb).astype(BF16)

    nt = (((1,), (1,)), ((), ()))
    s_in = lax.broadcasted_iota(jnp.int32, (S5_ROW, S5_COLS), 0) & (S5_CHUNK - 1)
    t_out = lax.broadcasted_iota(jnp.int32, (S5_ROW, S5_COLS), 1) & (S5_CHUNK - 1)
    for c in range(S5_ROW // S5_COLS):
        cols = slice(c * S5_COLS, (c + 1) * S5_COLS)
        t_f = lax.dot_general(tab_ref[_K_BP], tab_ref[_K_CP, cols, :], nt, preferred_element_type=F32)
        t_b = lax.dot_general(tab_ref[_N_KIND + _K_BP], tab_ref[_N_KIND + _K_CP, cols, :], nt,
                              preferred_element_type=F32)
        toep_ref[:, cols] = (jnp.where(s_in <= t_out, t_f, 0.0) + jnp.where(s_in >= t_out, t_b, 0.0)).astype(BF16)

    u = jnp.concatenate([jnp.concatenate([u_ref[q] for u_ref in u_refs], axis=0) for q in range(S5_P)], axis=1)
    half = S5_ROW // 2
    y = jnp.concatenate([jnp.dot(u, toep_ref[:, 0:half], preferred_element_type=F32),
                         jnp.dot(u, toep_ref[:, half:S5_ROW], preferred_element_type=F32)], axis=1)

    for d in range(2):
        fwd = d == 0
        dist = pos_ref[...] if fwd else rem_ref[...]
        st = jnp.dot(u, tab_ref[d * _N_KIND + _K_BW], preferred_element_type=F32)
        for k in range(n_steps):
            step = 1 << k
            sh = pltpu.roll(st, step if fwd else rows - step, 0)
            sh = jnp.where(dist >= step, sh, 0.0)
            shs = pltpu.roll(sh, S5_N, 1)
            pa = al_ref[d * 2 * _SCAN_STEPS_MAX + k:d * 2 * _SCAN_STEPS_MAX + k + 1, :]
            pb = al_ref[d * 2 * _SCAN_STEPS_MAX + _SCAN_STEPS_MAX + k:d * 2 * _SCAN_STEPS_MAX + _SCAN_STEPS_MAX + k + 1, :]
            st = st + sh * pa + shs * pb
        inc = pltpu.roll(st, 1 if fwd else rows - 1, 0)
        inc = jnp.where(dist >= 1, inc, 0.0).astype(BF16)
        y = y + lax.dot_general(inc, tab_ref[d * _N_KIND + _K_CS], nt, preferred_element_type=F32)

    y = y + dsk_ref[...] * u.astype(F32)

    off = 0
    for o_ref, r in zip(o_refs, rows_per):
        for p in range(S5_P):
            o_ref[p] = y[off:off + r, p * S5_CHUNK:(p + 1) * S5_CHUNK].astype(o_ref.dtype)
        off += r


def _s5_tables(a_re, a_im, log_step, b_re, b_im, c_re, c_im, n_steps):
    dt = jnp.exp(log_step.astype(F32))[..., None]
    a_re = a_re.astype(F32)
    a_im = a_im.astype(F32)
    lr, li = a_re * dt, a_im * dt
    er = jnp.exp(lr)
    abr, abi = er * jnp.cos(li), er * jnp.sin(li)
    den = a_re * a_re + a_im * a_im
    fr = ((abr - 1.0) * a_re + abi * a_im) / den
    fi = (abi * a_re - (abr - 1.0) * a_im) / den
    b_re = b_re.astype(F32)
    b_im = b_im.astype(F32)
    bbr = fr[..., None] * b_re - fi[..., None] * b_im
    bbi = fr[..., None] * b_im + fi[..., None] * b_re
    bbr, bbi = jnp.swapaxes(bbr, -1, -2), jnp.swapaxes(bbi, -1, -2)
    cr, ci = c_re.astype(F32), c_im.astype(F32)
    cat = lambda lo, hi: jnp.concatenate([lo, hi], axis=-1)
    yt = jnp.stack([cat(bbr, bbr), cat(-bbi, bbi), cat(cr, -cr), cat(-ci, -ci)], axis=2)
    yt = jnp.swapaxes(yt, 0, 1).reshape(S5_GROUPS, 2 * 4, S5_P, LANES)

    idx = jnp.arange(S5_CHUNK, dtype=F32)
    length = float(S5_CHUNK)
    expo = jnp.stack([
        jnp.stack([-idx, idx, length - 1.0 - idx, idx + 1.0]),
        jnp.stack([idx, -idx, idx, length - idx]),
    ])
    e = expo[:, None, :, :, None]
    mag = jnp.exp(lr[:, :, None, None, :] * e)
    ang = li[:, :, None, None, :] * e
    pr, pi = mag * jnp.cos(ang), mag * jnp.sin(ang)
    pt = cat(pr, pi)
    pt = jnp.swapaxes(pt, 0, 1).reshape(S5_GROUPS, 2 * _N_KIND, S5_CHUNK, LANES)

    steps = length * (2.0 ** jnp.arange(_SCAN_STEPS_MAX, dtype=F32))
    se = steps[None, None, :, None]
    smag = jnp.exp(lr[:, :, None, :] * se)
    sang = li[:, :, None, :] * se
    sr, si = smag * jnp.cos(sang), smag * jnp.sin(sang)
    al = jnp.concatenate([cat(sr, sr), cat(-si, si)], axis=2)
    al = jnp.swapaxes(al, 0, 1).reshape(S5_GROUPS, 2 * 2 * _SCAN_STEPS_MAX, LANES)
    del n_steps
    return pt, yt, al


def _s5(ut_list, shapes, a_re, a_im, log_step, b_re, b_im, c_re, c_im, d_skip):
    dsk = jnp.repeat(d_skip.astype(F32).reshape(S5_GROUPS, S5_P), S5_CHUNK, axis=1).reshape(S5_GROUPS, 1, S5_ROW)
    pos_parts, rem_parts, rows_per, max_cps = [], [], [], 1
    for (b, s) in shapes:
        cps = s // S5_CHUNK
        max_cps = max(max_cps, cps)
        p = jnp.tile(jnp.arange(cps, dtype=jnp.int32), b)
        pos_parts.append(p)
        rem_parts.append(cps - 1 - p)
        rows_per.append(b * cps)
    n_steps = max(1, (max_cps - 1).bit_length())
    rows = sum(rows_per)
    pos = jnp.broadcast_to(jnp.concatenate(pos_parts)[:, None], (rows, LANES))
    rem = jnp.broadcast_to(jnp.concatenate(rem_parts)[:, None], (rows, LANES))
    pt, yt, al = _s5_tables(a_re, a_im, log_step, b_re, b_im, c_re, c_im, n_steps)

    u4 = [ut.reshape(S5_GROUPS, S5_P, r, S5_CHUNK) for ut, r in zip(ut_list, rows_per)]
    io_specs = [pl.BlockSpec((None, S5_P, r, S5_CHUNK), lambda g: (g, 0, 0, 0)) for r in rows_per]
    outs = pl.pallas_call(
        functools.partial(_s5_kernel, rows_per=tuple(rows_per), n_steps=n_steps),
        grid=(S5_GROUPS,),
        in_specs=io_specs + [
            pl.BlockSpec((None, 2 * _N_KIND, S5_CHUNK, LANES), lambda g: (g, 0, 0, 0)),
            pl.BlockSpec((None, 2 * 4, S5_P, LANES), lambda g: (g, 0, 0, 0)),
            pl.BlockSpec((None, 2 * 2 * _SCAN_STEPS_MAX, LANES), lambda g: (g, 0, 0)),
            pl.BlockSpec((None, 1, S5_ROW), lambda g: (g, 0, 0)),
            pl.BlockSpec((rows, LANES), lambda g: (0, 0)),
            pl.BlockSpec((rows, LANES), lambda g: (0, 0)),
        ],
        out_specs=io_specs,
        out_shape=[jax.ShapeDtypeStruct((S5_GROUPS, S5_P, r, S5_CHUNK), BF16) for r in rows_per],
        scratch_shapes=[pltpu.VMEM((2 * _N_KIND, S5_ROW, LANES), BF16), pltpu.VMEM((S5_ROW, S5_ROW), BF16)],
        compiler_params=_params(("parallel",)),
        name="s5_chunked",
    )(*u4, pt, yt, al, dsk, pos, rem)
    return [o.reshape(D_MODEL, r * S5_CHUNK) for o, r in zip(outs, rows_per)]


def _glu_kernel(x_ref, yt_ref, wa_ref, wb_ref, o_ref, a_ref):
    j = pl.program_id(1)

    @pl.when(j == 0)
    def _():
        def rows(c, carry):
            r = pl.ds(pl.multiple_of(c * ROW_CHUNK, ROW_CHUNK), ROW_CHUNK)
            t = yt_ref[:, r].astype(F32).T
            gelu = 0.5 * t * (1.0 + jnp.tanh(math.sqrt(2.0 / math.pi) * (t + 0.044715 * (t * t * t))))
            a_ref[r, :] = gelu.astype(BF16)
            return carry
        lax.fori_loop(0, a_ref.shape[0] // ROW_CHUNK, rows, 0)

    a = a_ref[...]
    ga = jnp.dot(a, wa_ref[...], preferred_element_type=F32)
    gb = jnp.dot(a, wb_ref[...], preferred_element_type=F32)
    o_ref[...] = x_ref[...] + ga * jax.nn.sigmoid(gb)


def _glu(x2, yt, w_bf):
    t, d = x2.shape
    tm = min(1024, t)
    tn = 512
    nj = d // tn
    return pl.pallas_call(
        _glu_kernel,
        grid=(t // tm, nj),
        in_specs=[
            pl.BlockSpec((tm, tn), lambda i, j: (i, j)),
            pl.BlockSpec((d, tm), lambda i, j: (0, i)),
            pl.BlockSpec((d, tn), lambda i, j: (0, j)),
            pl.BlockSpec((d, tn), lambda i, j: (0, j + nj)),
        ],
        out_specs=pl.BlockSpec((tm, tn), lambda i, j: (i, j)),
        out_shape=jax.ShapeDtypeStruct((t, d), F32),
        scratch_shapes=[pltpu.VMEM((tm, d), BF16)],
        compiler_params=_params(("parallel", "arbitrary")),
        name="s5_glu",
    )(x2, yt, w_bf, w_bf)


def _route_kernel(x_ref, g_ref, wr_ref, idx_ref, gate_ref):
    h = _rms(x_ref[...], g_ref[...])
    w = wr_ref[...]
    h_hi, w_hi = h.astype(BF16), w.astype(BF16)
    h_lo = (h - h_hi.astype(F32)).astype(BF16)
    w_lo = (w - w_hi.astype(F32)).astype(BF16)
    logits = (jnp.dot(h_hi, w_hi, preferred_element_type=F32) + jnp.dot(h_lo, w_hi, preferred_element_type=F32)
              + jnp.dot(h_hi, w_lo, preferred_element_type=F32))
    lane = lax.broadcasted_iota(jnp.int32, logits.shape, 1)
    neg = jnp.float32(-jnp.inf)
    lg = jnp.where(lane < N_EXPERTS, logits, neg)
    m1 = jnp.max(lg, axis=-1, keepdims=True)
    i1 = jnp.min(jnp.where(lg == m1, lane, LANES), axis=-1, keepdims=True)
    lg2 = jnp.where(lane == i1, neg, lg)
    m2 = jnp.max(lg2, axis=-1, keepdims=True)
    i2 = jnp.min(jnp.where(lg2 == m2, lane, LANES), axis=-1, keepdims=True)
    ex = jnp.exp(m2 - m1)
    den = 1.0 + ex
    idx_ref[...] = jnp.where(lane == 0, i1, jnp.where(lane == 1, i2, 0))
    gate_ref[...] = jnp.where(lane == 0, 1.0 / den, jnp.where(lane == 1, ex / den, 0.0))


def _route(x2, g, w_router):
    t, d = x2.shape
    tm = min(512, t)
    wr = jnp.zeros((d, LANES), F32).at[:, :N_EXPERTS].set(w_router.astype(F32))
    return pl.pallas_call(
        _route_kernel,
        grid=(t // tm,),
        in_specs=[
            pl.BlockSpec((tm, d), lambda i: (i, 0)),
            pl.BlockSpec((1, d), lambda i: (0, 0)),
            pl.BlockSpec((d, LANES), lambda i: (0, 0)),
        ],
        out_specs=[pl.BlockSpec((tm, LANES), lambda i: (i, 0)), pl.BlockSpec((tm, LANES), lambda i: (i, 0))],
        out_shape=[jax.ShapeDtypeStruct((t, LANES), jnp.int32), jax.ShapeDtypeStruct((t, LANES), F32)],
        compiler_params=_params(("parallel",)),
        name="moe_route",
    )(x2, g.reshape(1, d), wr)


def _expert_kernel(src_ref, te_ref, tv_ref, x_hbm, g_ref, wg_ref, wu_ref, wd_ref, o_ref, xbuf, hb_ref, sem,
                   *, tm, n_tiles):
    i = pl.program_id(0)
    f = pl.program_id(1)

    def row_copy(tile, r):
        tok = src_ref[tile * tm + r]
        return pltpu.make_async_copy(x_hbm.at[pl.ds(tok, 1)], xbuf.at[pl.ds(r, 1)], sem.at[0])

    def start_gather(tile):
        def go(r, c):
            row_copy(tile, r).start()
            return c
        lax.fori_loop(0, tm, go, 0, unroll=DMA_UNROLL)

    def wait_gather(tile):
        def go(r, c):
            row_copy(tile, r).wait()
            return c
        lax.fori_loop(0, tm, go, 0, unroll=DMA_UNROLL)

    @pl.when(f == 0)
    def _():
        @pl.when(i == 0)
        def _():
            start_gather(0)

        wait_gather(i)
        chunk = min(ROW_CHUNK, tm)

        def rows(c, carry):
            r = pl.ds(pl.multiple_of(c * chunk, chunk), chunk)
            hb_ref[r, :] = _rms(xbuf[r, :], g_ref[...]).astype(BF16)
            return carry
        lax.fori_loop(0, tm // chunk, rows, 0)

        @pl.when(i + 1 < n_tiles)
        def _():
            start_gather(i + 1)

        o_ref[...] = jnp.zeros(o_ref.shape, F32)

    def swiglu(rows):
        h = hb_ref[0:rows, :]
        gate = jnp.dot(h, wg_ref[...], preferred_element_type=F32)
        up = jnp.dot(h, wu_ref[...], preferred_element_type=F32)
        act = (gate * jax.nn.sigmoid(gate) * up).astype(BF16)
        o_ref[0:rows, :] += jnp.dot(act, wd_ref[...], preferred_element_type=F32)

    n_valid = tv_ref[i]

    @pl.when(n_valid > tm // 2)
    def _():
        swiglu(tm)

    @pl.when((n_valid > 0) & (n_valid <= tm // 2))
    def _():
        swiglu(tm // 2)


def _combine_kernel(dest_ref, x_ref, gate_ref, gf_ref, y_hbm, o_ref, ybuf, sem, *, tm, n_tiles):
    i = pl.program_id(0)
    slot = i % 2

    def row_copy(tile, r, k, buf):
        row = dest_ref[(tile * tm + r) * 2 + k]
        return pltpu.make_async_copy(y_hbm.at[pl.ds(row, 1)], ybuf.at[buf, k, pl.ds(r, 1)], sem.at[buf])

    def start_gather(tile, buf):
        def go(r, c):
            row_copy(tile, r, 0, buf).start()
            row_copy(tile, r, 1, buf).start()
            return c
        lax.fori_loop(0, tm, go, 0, unroll=DMA_UNROLL)

    def wait_gather(tile, buf):
        def go(r, c):
            row_copy(tile, r, 0, buf).wait()
            row_copy(tile, r, 1, buf).wait()
            return c
        lax.fori_loop(0, tm, go, 0, unroll=DMA_UNROLL)

    @pl.when(i == 0)
    def _():
        start_gather(0, 0)

    @pl.when(i + 1 < n_tiles)
    def _():
        start_gather(i + 1, 1 - slot)

    wait_gather(i, slot)
    gate = gate_ref[...]
    y = x_ref[...] + gate[:, 0:1] * ybuf[slot, 0] + gate[:, 1:2] * ybuf[slot, 1]
    o_ref[...] = _rms(y, gf_ref[...])


def _moe(x2, g, w_router, wgu_bf, wd_bf, g_final):
    t, d = x2.shape
    tm = MOE_TILE if t >= 8 * MOE_TILE else 64
    tf = MOE_FF_TILE
    nf = FF_EXPERT // tf
    ridx, rgate = _route(x2, g, w_router)

    n_assign = 2 * t
    expert = ridx[:, 0:2].reshape(n_assign)
    onehot = (expert[:, None] == jnp.arange(N_EXPERTS, dtype=jnp.int32)[None, :]).astype(jnp.int32)
    csum = jnp.cumsum(onehot, axis=0)
    rank = jnp.sum((csum - onehot) * onehot, axis=1)
    padded = ((csum[-1] + tm - 1) // tm) * tm
    pend = jnp.cumsum(padded)
    dest = jnp.sum(onehot * (pend - padded)[None, :], axis=1) + rank
    n_slots = n_assign + N_EXPERTS * tm
    n_tiles = n_slots // tm
    src = jnp.zeros((n_slots,), jnp.int32).at[dest].set(jnp.arange(n_assign, dtype=jnp.int32) // 2)
    tile_start = jnp.arange(n_tiles, dtype=jnp.int32) * tm
    tile_expert = jnp.minimum(jnp.sum((tile_start[:, None] >= pend[None, :]).astype(jnp.int32), axis=1),
                              N_EXPERTS - 1)
    counts = csum[-1]
    group_end = (pend - padded + counts)[tile_expert]
    tile_valid = jnp.where(tile_start < pend[-1], jnp.clip(group_end - tile_start, 0, tm), 0).astype(jnp.int32)

    wcol = lambda i, f, tv: jnp.where(tv[i] != 0, f, nf - 1)
    ys = pl.pallas_call(
        functools.partial(_expert_kernel, tm=tm, n_tiles=n_tiles),
        grid_spec=pltpu.PrefetchScalarGridSpec(
            num_scalar_prefetch=3,
            grid=(n_tiles, nf),
            in_specs=[
                pl.BlockSpec(memory_space=pl.ANY),
                pl.BlockSpec((1, d), lambda i, f, src, te, tv: (0, 0)),
                pl.BlockSpec((None, d, tf), lambda i, f, src, te, tv: (te[i], 0, wcol(i, f, tv))),
                pl.BlockSpec((None, d, tf), lambda i, f, src, te, tv: (te[i], 0, wcol(i, f, tv) + nf)),
                pl.BlockSpec((None, tf, d), lambda i, f, src, te, tv: (te[i], wcol(i, f, tv), 0)),
            ],
            out_specs=pl.BlockSpec((tm, d), lambda i, f, src, te, tv: (i, 0)),
            scratch_shapes=[
                pltpu.VMEM((tm, d), F32),
                pltpu.VMEM((tm, d), BF16),
                pltpu.SemaphoreType.DMA((1,)),
            ],
        ),
        out_shape=jax.ShapeDtypeStruct((n_slots, d), F32),
        compiler_params=_params(("arbitrary", "arbitrary")),
        name="moe_experts",
    )(src, tile_expert, tile_valid, x2, g.reshape(1, d), wgu_bf, wgu_bf, wd_bf)

    tc = min(COMBINE_TILE, t)
    nc = t // tc
    return pl.pallas_call(
        functools.partial(_combine_kernel, tm=tc, n_tiles=nc),
        grid_spec=pltpu.PrefetchScalarGridSpec(
            num_scalar_prefetch=1,
            grid=(nc,),
            in_specs=[
                pl.BlockSpec((tc, d), lambda i, dst: (i, 0)),
                pl.BlockSpec((tc, LANES), lambda i, dst: (i, 0)),
                pl.BlockSpec((1, d), lambda i, dst: (0, 0)),
                pl.BlockSpec(memory_space=pl.ANY),
            ],
            out_specs=pl.BlockSpec((tc, d), lambda i, dst: (i, 0)),
            scratch_shapes=[
                pltpu.VMEM((2, 2, tc, d), F32),
                pltpu.SemaphoreType.DMA((2,)),
            ],
        ),
        out_shape=jax.ShapeDtypeStruct((t, d), F32),
        compiler_params=_params(("arbitrary",)),
        name="moe_combine",
    )(dest, x2, rgate, g_final.reshape(1, d), ys)


def _trunks(xs, g_mix_e, w_in_e, lambda_qk, subln_g, conv_w, w_out_e, g_ffn_e, w_gate_up, w_down,
            g_mix_o, s5_a_re, s5_a_im, s5_log_step, s5_b_re, s5_b_im, s5_c_re, s5_c_im, s5_d,
            w_glu, g_ffn_o, w_router, w_exp_gate_up, w_exp_down, g_final):
    w_in = w_in_e[0].astype(BF16)
    w_out = w_out_e[0].astype(BF16)
    w_gu = w_gate_up[0].astype(BF16)
    w_dn = w_down[0].astype(BF16)
    w_gl = w_glu[0].astype(BF16)
    w_egu = w_exp_gate_up[0].astype(BF16)
    w_edn = w_exp_down[0].astype(BF16)

    shapes = [(x.shape[0], x.shape[1]) for x in xs]
    x1s, us = [], []
    for x in xs:
        b, s, d = x.shape
        z = _in_proj(x.reshape(b * s, d), g_mix_e[0], w_in, s).reshape(b, s, IN_COLS)
        a = _attention(z, lambda_qk[0], subln_g[0])
        x1 = _out_proj(a, z, conv_w[0], x, w_out).reshape(b * s, d)
        x1 = _ffn(x1, g_ffn_e[0], w_gu, w_dn)
        x1s.append(x1)
        us.append(_norm_t(x1, g_mix_o[0]))
    ys = _s5(us, shapes, s5_a_re[0], s5_a_im[0], s5_log_step[0], s5_b_re[0], s5_b_im[0], s5_c_re[0], s5_c_im[0],
             s5_d[0])
    outs = []
    for x1, y, (b, s) in zip(x1s, ys, shapes):
        x2 = _glu(x1, y, w_gl)
        out = _moe(x2, g_ffn_o[0], w_router[0], w_egu, w_edn, g_final)
        outs.append(out.reshape(b, s, D_MODEL))
    return outs


def kernel(x_prompt, x_sample, g_mix_e, w_in_e, lambda_qk, subln_g, conv_w, w_out_e, g_ffn_e, w_gate_up, w_down, g_mix_o, s5_a_re, s5_a_im, s5_log_step, s5_b_re, s5_b_im, s5_c_re, s5_c_im, s5_d, w_glu, g_ffn_o, w_router, w_exp_gate_up, w_exp_down, g_final):
    y_prompt, y_sample = _trunks(
        [x_prompt, x_sample], g_mix_e, w_in_e, lambda_qk, subln_g, conv_w, w_out_e, g_ffn_e, w_gate_up, w_down,
        g_mix_o, s5_a_re, s5_a_im, s5_log_step, s5_b_re, s5_b_im, s5_c_re, s5_c_im, s5_d,
        w_glu, g_ffn_o, w_router, w_exp_gate_up, w_exp_down, g_final)
    return (y_prompt, y_sample)
```

```python
import functools
import math

import jax
import jax.numpy as jnp
from jax import lax
from jax.experimental import pallas as pl
from jax.experimental.pallas import tpu as pltpu

F32 = jnp.float32
BF16 = jnp.bfloat16

D_MODEL = 2048
ATTN_WIDTH = D_MODEL // 2
CONV_CH = D_MODEL - ATTN_WIDTH
N_HEADS = 8
QK_DIM = 64
V_DIM = 2 * QK_DIM
ROPE_DIM = QK_DIM // 4
ROPE_THETA = 500000.0
IN_COLS = 3 * ATTN_WIDTH + 3 * CONV_CH
S5_P = 16
S5_GROUPS = D_MODEL // S5_P
S5_N = 64
FF_DENSE = 5632
N_EXPERTS = 8
FF_EXPERT = 7168
RMS_EPS = 1e-5
LAMBDA_INIT_0 = 0.8 - 0.6 * math.exp(-0.3 * 0)
QK_SCALE_LOG2 = QK_DIM ** -0.5 * math.log2(math.e)

LANES = 128
SUBLANES = 8
ROW_CHUNK = 128
ATTN_SHORT_SEQ = 4096
ATTN_UNROLL = 5
MOE_TILE = 1024
MOE_FF_TILE = 512
COMBINE_TILE = 256
DMA_UNROLL = 8
S5_CHUNK = 128
S5_COLS = 512
S5_ROW = S5_CHUNK * S5_P
VMEM_LIMIT = 52 * 1024 * 1024


def _params(sem):
    return pltpu.CompilerParams(dimension_semantics=sem, vmem_limit_bytes=VMEM_LIMIT)


def _rms(x, g):
    ms = jnp.mean(x * x, axis=-1, keepdims=True)
    return x * lax.rsqrt(ms + RMS_EPS) * g


def _inproj_kernel(x_ref, g_ref, w_ref, ra_ref, rm_ref, rp_ref, o_ref, h_ref, *, tn, n_q, n_rope):
    j = pl.program_id(1)

    @pl.when(j == 0)
    def _():
        h_ref[...] = _rms(x_ref[...], g_ref[...]).astype(BF16)

    acc = jnp.dot(h_ref[...], w_ref[...], preferred_element_type=F32)

    @pl.when(j < n_rope)
    def _():
        scale = jnp.where(j < n_q, QK_SCALE_LOG2, 1.0).astype(F32)
        a = ra_ref[...] * scale
        bm = rm_ref[...] * scale
        bp = rp_ref[...] * scale
        for c in range(tn // LANES):
            t = acc[:, c * LANES:(c + 1) * LANES]
            tb = t.astype(BF16)
            r = (t * a + pltpu.roll(tb, LANES - ROPE_DIM // 2, 1).astype(F32) * bm
                 + pltpu.roll(tb, ROPE_DIM // 2, 1).astype(F32) * bp)
            o_ref[:, c * LANES:(c + 1) * LANES] = r.astype(o_ref.dtype)

    @pl.when(j >= n_rope)
    def _():
        o_ref[...] = acc.astype(o_ref.dtype)


def _rope_tables(seq):
    half = ROPE_DIM // 2
    inv = 1.0 / (ROPE_THETA ** (jnp.arange(0, ROPE_DIM, 2, dtype=F32) / ROPE_DIM))
    ang = jnp.arange(seq, dtype=F32)[:, None] * inv[None, :]
    cos, sin = jnp.cos(ang), jnp.sin(ang)
    ones = jnp.ones((seq, QK_DIM - ROPE_DIM), F32)
    zeros = jnp.zeros((seq, QK_DIM - ROPE_DIM), F32)
    zh = jnp.zeros((seq, half), F32)
    a = jnp.concatenate([cos, cos, ones], axis=-1)
    bm = jnp.concatenate([-sin, zh, zeros], axis=-1)
    bp = jnp.concatenate([zh, sin, zeros], axis=-1)
    rep = LANES // QK_DIM
    return jnp.tile(a, (1, rep)), jnp.tile(bm, (1, rep)), jnp.tile(bp, (1, rep))


def _in_proj(x2, g, w_bf, seq):
    t, d = x2.shape
    tm = min(1024, seq)
    tn = 1024
    ra, rm, rp = _rope_tables(seq)
    nblk = seq // tm
    rope_spec = pl.BlockSpec((tm, LANES), lambda i, j: (i % nblk, 0))
    return pl.pallas_call(
        functools.partial(_inproj_kernel, tn=tn, n_q=ATTN_WIDTH // tn, n_rope=2 * ATTN_WIDTH // tn),
        grid=(t // tm, IN_COLS // tn),
        in_specs=[
            pl.BlockSpec((tm, d), lambda i, j: (i, 0)),
            pl.BlockSpec((1, d), lambda i, j: (0, 0)),
            pl.BlockSpec((d, tn), lambda i, j: (0, j)),
            rope_spec, rope_spec, rope_spec,
        ],
        out_specs=pl.BlockSpec((tm, tn), lambda i, j: (i, j)),
        out_shape=jax.ShapeDtypeStruct((t, IN_COLS), BF16),
        scratch_shapes=[pltpu.VMEM((tm, d), BF16)],
        compiler_params=_params(("parallel", "arbitrary")),
        name="in_proj",
    )(x2, g.reshape(1, d), w_bf, ra, rm, rp)


def _attn_kernel(q_ref, k_ref, v_ref, lq_ref, sg_ref, o_ref, qq_ref, vt_ref, s0_ref, s1_ref, acc_ref,
                 *, tq, tk, nk):
    nq = 2 * tq

    @pl.when(pl.program_id(2) == 0)
    def _():
        def flip(c, carry):
            start = pl.multiple_of(c * tk, tk)
            vt_ref[:, pl.ds(start, tk)] = v_ref[pl.ds(start, tk), :].astype(F32).T.astype(BF16)
            return carry
        lax.fori_loop(0, nk, flip, 0)

    qt = q_ref[...].astype(F32).T.astype(BF16)
    dim = lax.broadcasted_iota(jnp.int32, (V_DIM, tq), 0)
    zero = jnp.zeros_like(qt)
    qq_ref[:, 0:tq] = jnp.where(dim < QK_DIM, qt, zero)
    qq_ref[:, tq:nq] = jnp.where(dim >= QK_DIM, qt, zero)
    acc_ref[...] = jnp.zeros(acc_ref.shape, F32)

    def scores(j, s_ref):
        start = pl.multiple_of(j * tk, tk)
        s = jnp.dot(k_ref[pl.ds(start, tk), :], qq_ref[...], preferred_element_type=F32)
        s_ref[...] = s
        return jnp.max(s, axis=0, keepdims=True)

    def accumulate(j, s_ref, carry, block_max):
        m_prev, l_prev = carry
        m_new = jnp.maximum(m_prev, block_max)
        alpha = jnp.exp2(m_prev - m_new)
        p = jnp.exp2(s_ref[...] - m_new)
        l_new = alpha * l_prev + jnp.sum(p, axis=0, keepdims=True)
        start = pl.multiple_of(j * tk, tk)
        pv = jnp.dot(vt_ref[:, pl.ds(start, tk)], p.astype(BF16), preferred_element_type=F32)
        acc_ref[...] = alpha * acc_ref[...] + pv
        return m_new, l_new

    def body(jj, carry):
        ml, max_even = carry
        j = 2 * jj
        max_odd = scores(j + 1, s1_ref)
        ml = accumulate(j, s0_ref, ml, max_even)
        max_even = scores(j + 2, s0_ref)
        ml = accumulate(j + 1, s1_ref, ml, max_odd)
        return ml, max_even

    ml0 = (jnp.full((1, nq), -jnp.inf, F32), jnp.zeros((1, nq), F32))
    ml, max_even = lax.fori_loop(0, nk // 2 - 1, body, (ml0, scores(0, s0_ref)), unroll=ATTN_UNROLL)
    max_odd = scores(nk - 1, s1_ref)
    ml = accumulate(nk - 2, s0_ref, ml, max_even)
    _, denom = accumulate(nk - 1, s1_ref, ml, max_odd)

    lq = lq_ref[...]
    lam = (jnp.exp(jnp.sum(lq[0:1, :] * lq[1:2, :], axis=-1, keepdims=True))
           - jnp.exp(jnp.sum(lq[2:3, :] * lq[3:4, :], axis=-1, keepdims=True)) + LAMBDA_INIT_0)
    ot = acc_ref[...] / denom
    diff = ot[:, 0:tq] - lam * ot[:, tq:nq]
    ms = jnp.mean(diff * diff, axis=0, keepdims=True)
    y = diff * lax.rsqrt(ms + RMS_EPS) * sg_ref[...] * (1.0 - LAMBDA_INIT_0)
    o_ref[...] = y.T.astype(o_ref.dtype)


def _attention(z3, lambda_qk, subln_g):
    b, s, _ = z3.shape
    tq = min(1024 if s <= ATTN_SHORT_SEQ else 512, s)
    tk = min(512, s // 2)
    assert s % tq == 0 and s % (2 * tk) == 0, "key chunks are processed in pairs"
    kcol = ATTN_WIDTH // V_DIM
    return pl.pallas_call(
        functools.partial(_attn_kernel, tq=tq, tk=tk, nk=s // tk),
        grid=(b, N_HEADS, s // tq),
        in_specs=[
            pl.BlockSpec((None, tq, V_DIM), lambda bi, h, i: (bi, i, h)),
            pl.BlockSpec((None, s, V_DIM), lambda bi, h, i: (bi, 0, kcol + h)),
            pl.BlockSpec((None, s, V_DIM), lambda bi, h, i: (bi, 0, 2 * kcol + h)),
            pl.BlockSpec((4, QK_DIM), lambda bi, h, i: (0, 0)),
            pl.BlockSpec((V_DIM, 1), lambda bi, h, i: (0, 0)),
        ],
        out_specs=pl.BlockSpec((None, tq, V_DIM), lambda bi, h, i: (bi, i, h)),
        out_shape=jax.ShapeDtypeStruct((b, s, ATTN_WIDTH), BF16),
        scratch_shapes=[
            pltpu.VMEM((V_DIM, 2 * tq), BF16),
            pltpu.VMEM((V_DIM, s), BF16),
            pltpu.VMEM((tk, 2 * tq), F32),
            pltpu.VMEM((tk, 2 * tq), F32),
            pltpu.VMEM((V_DIM, 2 * tq), F32),
        ],
        compiler_params=_params(("parallel", "parallel", "arbitrary")),
        name="diff_attention",
    )(z3, z3, z3, lambda_qk.astype(F32), subln_g.reshape(V_DIM, 1).astype(F32))


def _outproj_kernel(a_ref, bg_ref, cg_ref, xc_ref, cgp_ref, xcp_ref, cgn_ref, xcn_ref, cw_ref, x_ref, w_ref,
                    o_ref, *, tm, nt):
    i = pl.program_id(1)
    u = cg_ref[...].astype(F32) * xc_ref[...].astype(F32)
    u_before = cgp_ref[SUBLANES - 1:SUBLANES, :].astype(F32) * xcp_ref[SUBLANES - 1:SUBLANES, :].astype(F32)
    u_after = cgn_ref[0:1, :].astype(F32) * xcn_ref[0:1, :].astype(F32)
    u_before = u_before * (i > 0).astype(F32)
    u_after = u_after * (i < nt - 1).astype(F32)
    row = lax.broadcasted_iota(jnp.int32, u.shape, 0)
    u_prev = jnp.where(row == 0, u_before, pltpu.roll(u, 1, 0))
    u_next = jnp.where(row == tm - 1, u_after, pltpu.roll(u, tm - 1, 0))
    cw = cw_ref[...]
    y = u_prev * cw[0:1, :] + u * cw[1:2, :] + u_next * cw[2:3, :]
    c = (bg_ref[...].astype(F32) * y).astype(BF16)
    acc = jnp.dot(a_ref[...], w_ref[0:ATTN_WIDTH, :], preferred_element_type=F32)
    acc = acc + jnp.dot(c, w_ref[ATTN_WIDTH:D_MODEL, :], preferred_element_type=F32)
    o_ref[...] = x_ref[...] + acc


def _out_proj(a3, z3, conv_w, x3, w_bf):
    b, s, _ = z3.shape
    tm = min(512, s)
    nt = s // tm
    rb = tm // SUBLANES
    last = s // SUBLANES - 1
    cb = 3 * ATTN_WIDTH // CONV_CH
    main = lambda col: pl.BlockSpec((None, tm, CONV_CH), lambda bi, i: (bi, i, col))
    prev = lambda col: pl.BlockSpec((None, SUBLANES, CONV_CH), lambda bi, i: (bi, jnp.maximum(i * rb - 1, 0), col))
    nxt = lambda col: pl.BlockSpec((None, SUBLANES, CONV_CH), lambda bi, i: (bi, jnp.minimum((i + 1) * rb, last), col))
    return pl.pallas_call(
        functools.partial(_outproj_kernel, tm=tm, nt=nt),
        grid=(b, nt),
        in_specs=[
            pl.BlockSpec((None, tm, ATTN_WIDTH), lambda bi, i: (bi, i, 0)),
            main(cb), main(cb + 1), main(cb + 2),
            prev(cb + 1), prev(cb + 2), nxt(cb + 1), nxt(cb + 2),
            pl.BlockSpec((3, CONV_CH), lambda bi, i: (0, 0)),
            pl.BlockSpec((None, tm, D_MODEL), lambda bi, i: (bi, i, 0)),
            pl.BlockSpec((D_MODEL, D_MODEL), lambda bi, i: (0, 0)),
        ],
        out_specs=pl.BlockSpec((None, tm, D_MODEL), lambda bi, i: (bi, i, 0)),
        out_shape=jax.ShapeDtypeStruct((b, s, D_MODEL), F32),
        compiler_params=_params(("parallel", "arbitrary")),
        name="out_proj_conv",
    )(a3, z3, z3, z3, z3, z3, z3, z3, conv_w.astype(F32), x3, w_bf)


def _ffn_kernel(x_ref, g_ref, wg_ref, wu_ref, wd_ref, o_ref, h_ref):
    f = pl.program_id(1)

    @pl.when(f == 0)
    def _():
        x = x_ref[...]
        h_ref[...] = _rms(x, g_ref[...]).astype(BF16)
        o_ref[...] = x

    h = h_ref[...]
    gate = jnp.dot(h, wg_ref[...], preferred_element_type=F32)
    up = jnp.dot(h, wu_ref[...], preferred_element_type=F32)
    act = (gate * jax.nn.sigmoid(gate) * up).astype(BF16)
    o_ref[...] += jnp.dot(act, wd_ref[...], preferred_element_type=F32)


def _ffn(x2, g, wgu_bf, wd_bf):
    t, d = x2.shape
    tm = min(512, t)
    tf = 512
    nf = FF_DENSE // tf
    return pl.pallas_call(
        _ffn_kernel,
        grid=(t // tm, nf),
        in_specs=[
            pl.BlockSpec((tm, d), lambda i, f: (i, 0)),
            pl.BlockSpec((1, d), lambda i, f: (0, 0)),
            pl.BlockSpec((d, tf), lambda i, f: (0, f)),
            pl.BlockSpec((d, tf), lambda i, f: (0, f + nf)),
            pl.BlockSpec((tf, d), lambda i, f: (f, 0)),
        ],
        out_specs=pl.BlockSpec((tm, d), lambda i, f: (i, 0)),
        out_shape=jax.ShapeDtypeStruct((t, d), F32),
        scratch_shapes=[pltpu.VMEM((tm, d), BF16)],
        compiler_params=_params(("parallel", "arbitrary")),
        name="ffn_swiglu",
    )(x2, g.reshape(1, d), wgu_bf, wgu_bf, wd_bf)


def _norm_kernel(x_ref, g_ref, o_ref):
    def rows(c, carry):
        r = pl.ds(pl.multiple_of(c * ROW_CHUNK, ROW_CHUNK), ROW_CHUNK)
        o_ref[:, r] = _rms(x_ref[r, :], g_ref[...]).T.astype(o_ref.dtype)
        return carry
    lax.fori_loop(0, x_ref.shape[0] // ROW_CHUNK, rows, 0)


def _norm_t(x2, g):
    t, d = x2.shape
    tm = min(512, t)
    return pl.pallas_call(
        _norm_kernel,
        grid=(t // tm,),
        in_specs=[pl.BlockSpec((tm, d), lambda i: (i, 0)), pl.BlockSpec((1, d), lambda i: (0, 0))],
        out_specs=pl.BlockSpec((d, tm), lambda i: (0, i)),
        out_shape=jax.ShapeDtypeStruct((d, t), BF16),
        compiler_params=_params(("parallel",)),
        name="s5_norm",
    )(x2, g.reshape(1, d))


_K_BP, _K_CP, _K_BW, _K_CS = 0, 1, 2, 3
_N_KIND = 4
_SCAN_STEPS_MAX = 16


def _s5_kernel(*refs, rows_per, n_steps):
    n_in = len(rows_per)
    u_refs = refs[:n_in]
    pt_ref, yt_ref, al_ref, dsk_ref, pos_ref, rem_ref = refs[n_in:n_in + 6]
    o_refs = refs[n_in + 6:2 * n_in + 6]
    tab_ref, toep_ref = refs[2 * n_in + 6:]
    rows = sum(rows_per)

    for d in range(2):
        for kind in range(_N_KIND):
            x = pt_ref[d * _N_KIND + kind]
            xs = pltpu.roll(x, S5_N, 1)
            par = 0 if kind in (_K_BP, _K_BW) else 2
            for q in range(S5_P):
                ya = yt_ref[d * 4 + par, q:q + 1, :]
                yb = yt_ref[d * 4 + par + 1, q:q + 1, :]
                tab_ref[d * _N_KIND + kind, q * S5_CHUNK:(q + 1) * S5_CHUNK, :] = (x * ya + xs * yb).astype(BF16)

    nt = (((1,), (1,)), ((), ()))
    s_in = lax.broadcasted_iota(jnp.int32, (S5_ROW, S5_COLS), 0) & (S5_CHUNK - 1)
    t_out = lax.broadcasted_iota(jnp.int32, (S5_ROW, S5_COLS), 1) & (S5_CHUNK - 1)
    for c in range(S5_ROW // S5_COLS):
        cols = slice(c * S5_COLS, (c + 1) * S5_COLS)
        t_f = lax.dot_general(tab_ref[_K_BP], tab_ref[_K_CP, cols, :], nt, preferred_element_type=F32)
        t_b = lax.dot_general(tab_ref[_N_KIND + _K_BP], tab_ref[_N_KIND + _K_CP, cols, :], nt,
                              preferred_element_type=F32)
        toep_ref[:, cols] = (jnp.where(s_in <= t_out, t_f, 0.0) + jnp.where(s_in >= t_out, t_b, 0.0)).astype(BF16)

    u = jnp.concatenate([jnp.concatenate([u_ref[q] for u_ref in u_refs], axis=0) for q in range(S5_P)], axis=1)
    half = S5_ROW // 2
    y = jnp.concatenate([jnp.dot(u, toep_ref[:, 0:half], preferred_element_type=F32),
                         jnp.dot(u, toep_ref[:, half:S5_ROW], preferred_element_type=F32)], axis=1)

    for d in range(2):
        fwd = d == 0
        dist = pos_ref[...] if fwd else rem_ref[...]
        st = jnp.dot(u, tab_ref[d * _N_KIND + _K_BW], preferred_element_type=F32)
        for k in range(n_steps):
            step = 1 << k
            sh = pltpu.roll(st, step if fwd else rows - step, 0)
            sh = jnp.where(dist >= step, sh, 0.0)
            shs = pltpu.roll(sh, S5_N, 1)
            pa = al_ref[d * 2 * _SCAN_STEPS_MAX + k:d * 2 * _SCAN_STEPS_MAX + k + 1, :]
            pb = al_ref[d * 2 * _SCAN_STEPS_MAX + _SCAN_STEPS_MAX + k:d * 2 * _SCAN_STEPS_MAX + _SCAN_STEPS_MAX + k + 1, :]
            st = st + sh * pa + shs * pb
        inc = pltpu.roll(st, 1 if fwd else rows - 1, 0)
        inc = jnp.where(dist >= 1, inc, 0.0).astype(BF16)
        y = y + lax.dot_general(inc, tab_ref[d * _N_KIND + _K_CS], nt, preferred_element_type=F32)

    y = y + dsk_ref[...] * u.astype(F32)

    off = 0
    for o_ref, r in zip(o_refs, rows_per):
        for p in range(S5_P):
            o_ref[p] = y[off:off + r, p * S5_CHUNK:(p + 1) * S5_CHUNK].astype(o_ref.dtype)
        off += r


def _s5_tables(a_re, a_im, log_step, b_re, b_im, c_re, c_im, n_steps):
    dt = jnp.exp(log_step.astype(F32))[..., None]
    a_re = a_re.astype(F32)
    a_im = a_im.astype(F32)
    lr, li = a_re * dt, a_im * dt
    er = jnp.exp(lr)
    abr, abi = er * jnp.cos(li), er * jnp.sin(li)
    den = a_re * a_re + a_im * a_im
    fr = ((abr - 1.0) * a_re + abi * a_im) / den
    fi = (abi * a_re - (abr - 1.0) * a_im) / den
    b_re = b_re.astype(F32)
    b_im = b_im.astype(F32)
    bbr = fr[..., None] * b_re - fi[..., None] * b_im
    bbi = fr[..., None] * b_im + fi[..., None] * b_re
    bbr, bbi = jnp.swapaxes(bbr, -1, -2), jnp.swapaxes(bbi, -1, -2)
    cr, ci = c_re.astype(F32), c_im.astype(F32)
    cat = lambda lo, hi: jnp.concatenate([lo, hi], axis=-1)
    yt = jnp.stack([cat(bbr, bbr), cat(-bbi, bbi), cat(cr, -cr), cat(-ci, -ci)], axis=2)
    yt = jnp.swapaxes(yt, 0, 1).reshape(S5_GROUPS, 2 * 4, S5_P, LANES)

    idx = jnp.arange(S5_CHUNK, dtype=F32)
    length = float(S5_CHUNK)
    mid = length / 2.0
    expo = jnp.stack([
        jnp.stack([mid - idx, idx - mid, length - 1.0 - idx, idx + 1.0]),
        jnp.stack([idx - mid, mid - idx, idx, length - idx]),
    ])
    e = expo[:, None, :, :, None]
    mag = jnp.exp(lr[:, :, None, None, :] * e)
    ang = li[:, :, None, None, :] * e
    pr, pi = mag * jnp.cos(ang), mag * jnp.sin(ang)
    pt = cat(pr, pi)
    pt = jnp.swapaxes(pt, 0, 1).reshape(S5_GROUPS, 2 * _N_KIND, S5_CHUNK, LANES)

    steps = length * (2.0 ** jnp.arange(_SCAN_STEPS_MAX, dtype=F32))
    se = steps[None, None, :, None]
    smag = jnp.exp(lr[:, :, None, :] * se)
    sang = li[:, :, None, :] * se
    sr, si = smag * jnp.cos(sang), smag * jnp.sin(sang)
    al = jnp.concatenate([cat(sr, sr), cat(-si, si)], axis=2)
    al = jnp.swapaxes(al, 0, 1).reshape(S5_GROUPS, 2 * 2 * _SCAN_STEPS_MAX, LANES)
    del n_steps
    return pt, yt, al


def _s5(ut_list, shapes, a_re, a_im, log_step, b_re, b_im, c_re, c_im, d_skip):
    dsk = jnp.repeat(d_skip.astype(F32).reshape(S5_GROUPS, S5_P), S5_CHUNK, axis=1).reshape(S5_GROUPS, 1, S5_ROW)
    pos_parts, rem_parts, rows_per, max_cps = [], [], [], 1
    for (b, s) in shapes:
        cps = s // S5_CHUNK
        max_cps = max(max_cps, cps)
        p = jnp.tile(jnp.arange(cps, dtype=jnp.int32), b)
        pos_parts.append(p)
        rem_parts.append(cps - 1 - p)
        rows_per.append(b * cps)
    n_steps = max(1, (max_cps - 1).bit_length())
    rows = sum(rows_per)
    pos = jnp.broadcast_to(jnp.concatenate(pos_parts)[:, None], (rows, LANES))
    rem = jnp.broadcast_to(jnp.concatenate(rem_parts)[:, None], (rows, LANES))
    pt, yt, al = _s5_tables(a_re, a_im, log_step, b_re, b_im, c_re, c_im, n_steps)

    u4 = [ut.reshape(S5_GROUPS, S5_P, r, S5_CHUNK) for ut, r in zip(ut_list, rows_per)]
    io_specs = [pl.BlockSpec((None, S5_P, r, S5_CHUNK), lambda g: (g, 0, 0, 0)) for r in rows_per]
    outs = pl.pallas_call(
        functools.partial(_s5_kernel, rows_per=tuple(rows_per), n_steps=n_steps),
        grid=(S5_GROUPS,),
        in_specs=io_specs + [
            pl.BlockSpec((None, 2 * _N_KIND, S5_CHUNK, LANES), lambda g: (g, 0, 0, 0)),
            pl.BlockSpec((None, 2 * 4, S5_P, LANES), lambda g: (g, 0, 0, 0)),
            pl.BlockSpec((None, 2 * 2 * _SCAN_STEPS_MAX, LANES), lambda g: (g, 0, 0)),
            pl.BlockSpec((None, 1, S5_ROW), lambda g: (g, 0, 0)),
            pl.BlockSpec((rows, LANES), lambda g: (0, 0)),
            pl.BlockSpec((rows, LANES), lambda g: (0, 0)),
        ],
        out_specs=io_specs,
        out_shape=[jax.ShapeDtypeStruct((S5_GROUPS, S5_P, r, S5_CHUNK), BF16) for r in rows_per],
        scratch_shapes=[pltpu.VMEM((2 * _N_KIND, S5_ROW, LANES), BF16), pltpu.VMEM((S5_ROW, S5_ROW), BF16)],
        compiler_params=_params(("parallel",)),
        name="s5_chunked",
    )(*u4, pt, yt, al, dsk, pos, rem)
    return [o.reshape(D_MODEL, r * S5_CHUNK) for o, r in zip(outs, rows_per)]


def _glu_kernel(x_ref, yt_ref, wa_ref, wb_ref, o_ref, a_ref):
    j = pl.program_id(1)

    @pl.when(j == 0)
    def _():
        def rows(c, carry):
            r = pl.ds(pl.multiple_of(c * ROW_CHUNK, ROW_CHUNK), ROW_CHUNK)
            t = yt_ref[:, r].astype(F32).T
            gelu = 0.5 * t * (1.0 + jnp.tanh(math.sqrt(2.0 / math.pi) * (t + 0.044715 * (t * t * t))))
            a_ref[r, :] = gelu.astype(BF16)
            return carry
        lax.fori_loop(0, a_ref.shape[0] // ROW_CHUNK, rows, 0)

    a = a_ref[...]
    ga = jnp.dot(a, wa_ref[...], preferred_element_type=F32)
    gb = jnp.dot(a, wb_ref[...], preferred_element_type=F32)
    o_ref[...] = x_ref[...] + ga * jax.nn.sigmoid(gb)


def _glu(x2, yt, w_bf):
    t, d = x2.shape
    tm = min(1024, t)
    tn = 512
    nj = d // tn
    return pl.pallas_call(
        _glu_kernel,
        grid=(t // tm, nj),
        in_specs=[
            pl.BlockSpec((tm, tn), lambda i, j: (i, j)),
            pl.BlockSpec((d, tm), lambda i, j: (0, i)),
            pl.BlockSpec((d, tn), lambda i, j: (0, j)),
            pl.BlockSpec((d, tn), lambda i, j: (0, j + nj)),
        ],
        out_specs=pl.BlockSpec((tm, tn), lambda i, j: (i, j)),
        out_shape=jax.ShapeDtypeStruct((t, d), F32),
        scratch_shapes=[pltpu.VMEM((tm, d), BF16)],
        compiler_params=_params(("parallel", "arbitrary")),
        name="s5_glu",
    )(x2, yt, w_bf, w_bf)


def _route_kernel(x_ref, g_ref, wr_ref, idx_ref, gate_ref):
    h = _rms(x_ref[...], g_ref[...])
    w = wr_ref[...]
    h_hi, w_hi = h.astype(BF16), w.astype(BF16)
    h_lo = (h - h_hi.astype(F32)).astype(BF16)
    w_lo = (w - w_hi.astype(F32)).astype(BF16)
    logits = (jnp.dot(h_hi, w_hi, preferred_element_type=F32) + jnp.dot(h_lo, w_hi, preferred_element_type=F32)
              + jnp.dot(h_hi, w_lo, preferred_element_type=F32))
    lane = lax.broadcasted_iota(jnp.int32, logits.shape, 1)
    neg = jnp.float32(-jnp.inf)
    lg = jnp.where(lane < N_EXPERTS, logits, neg)
    m1 = jnp.max(lg, axis=-1, keepdims=True)
    i1 = jnp.min(jnp.where(lg == m1, lane, LANES), axis=-1, keepdims=True)
    lg2 = jnp.where(lane == i1, neg, lg)
    m2 = jnp.max(lg2, axis=-1, keepdims=True)
    i2 = jnp.min(jnp.where(lg2 == m2, lane, LANES), axis=-1, keepdims=True)
    ex = jnp.exp(m2 - m1)
    den = 1.0 + ex
    idx_ref[...] = jnp.where(lane == 0, i1, jnp.where(lane == 1, i2, 0))
    gate_ref[...] = jnp.where(lane == 0, 1.0 / den, jnp.where(lane == 1, ex / den, 0.0))


def _route(x2, g, w_router):
    t, d = x2.shape
    tm = min(512, t)
    wr = jnp.zeros((d, LANES), F32).at[:, :N_EXPERTS].set(w_router.astype(F32))
    return pl.pallas_call(
        _route_kernel,
        grid=(t // tm,),
        in_specs=[
            pl.BlockSpec((tm, d), lambda i: (i, 0)),
            pl.BlockSpec((1, d), lambda i: (0, 0)),
            pl.BlockSpec((d, LANES), lambda i: (0, 0)),
        ],
        out_specs=[pl.BlockSpec((tm, LANES), lambda i: (i, 0)), pl.BlockSpec((tm, LANES), lambda i: (i, 0))],
        out_shape=[jax.ShapeDtypeStruct((t, LANES), jnp.int32), jax.ShapeDtypeStruct((t, LANES), F32)],
        compiler_params=_params(("parallel",)),
        name="moe_route",
    )(x2, g.reshape(1, d), wr)


def _expert_kernel(src_ref, te_ref, tv_ref, x_hbm, g_ref, wg_ref, wu_ref, wd_ref, o_ref, xbuf, hb_ref, sem,
                   *, tm, n_tiles):
    i = pl.program_id(0)
    f = pl.program_id(1)

    def row_copy(tile, r):
        tok = src_ref[tile * tm + r]
        return pltpu.make_async_copy(x_hbm.at[pl.ds(tok, 1)], xbuf.at[pl.ds(r, 1)], sem.at[0])

    def start_gather(tile):
        def go(r, c):
            row_copy(tile, r).start()
            return c
        lax.fori_loop(0, tm, go, 0, unroll=DMA_UNROLL)

    def wait_gather(tile):
        def go(r, c):
            row_copy(tile, r).wait()
            return c
        lax.fori_loop(0, tm, go, 0, unroll=DMA_UNROLL)

    @pl.when(f == 0)
    def _():
        @pl.when(i == 0)
        def _():
            start_gather(0)

        wait_gather(i)
        chunk = min(ROW_CHUNK, tm)

        def rows(c, carry):
            r = pl.ds(pl.multiple_of(c * chunk, chunk), chunk)
            hb_ref[r, :] = _rms(xbuf[r, :], g_ref[...]).astype(BF16)
            return carry
        lax.fori_loop(0, tm // chunk, rows, 0)

        @pl.when(i + 1 < n_tiles)
        def _():
            start_gather(i + 1)

        o_ref[...] = jnp.zeros(o_ref.shape, F32)

    def swiglu(rows):
        h = hb_ref[0:rows, :]
        gate = jnp.dot(h, wg_ref[...], preferred_element_type=F32)
        up = jnp.dot(h, wu_ref[...], preferred_element_type=F32)
        act = (gate * jax.nn.sigmoid(gate) * up).astype(BF16)
        o_ref[0:rows, :] += jnp.dot(act, wd_ref[...], preferred_element_type=F32)

    n_valid = tv_ref[i]

    @pl.when(n_valid > tm // 2)
    def _():
        swiglu(tm)

    @pl.when((n_valid > 0) & (n_valid <= tm // 2))
    def _():
        swiglu(tm // 2)


def _combine_kernel(dest_ref, x_ref, gate_ref, gf_ref, y_hbm, o_ref, ybuf, sem, *, tm, n_tiles):
    i = pl.program_id(0)
    slot = i % 2

    def row_copy(tile, r, k, buf):
        row = dest_ref[(tile * tm + r) * 2 + k]
        return pltpu.make_async_copy(y_hbm.at[pl.ds(row, 1)], ybuf.at[buf, k, pl.ds(r, 1)], sem.at[buf])

    def start_gather(tile, buf):
        def go(r, c):
            row_copy(tile, r, 0, buf).start()
            row_copy(tile, r, 1, buf).start()
            return c
        lax.fori_loop(0, tm, go, 0, unroll=DMA_UNROLL)

    def wait_gather(tile, buf):
        def go(r, c):
            row_copy(tile, r, 0, buf).wait()
            row_copy(tile, r, 1, buf).wait()
            return c
        lax.fori_loop(0, tm, go, 0, unroll=DMA_UNROLL)

    @pl.when(i == 0)
    def _():
        start_gather(0, 0)

    @pl.when(i + 1 < n_tiles)
    def _():
        start_gather(i + 1, 1 - slot)

    wait_gather(i, slot)
    gate = gate_ref[...]
    y = x_ref[...] + gate[:, 0:1] * ybuf[slot, 0] + gate[:, 1:2] * ybuf[slot, 1]
    o_ref[...] = _rms(y, gf_ref[...])


def _moe(x2, g, w_router, wgu_bf, wd_bf, g_final):
    t, d = x2.shape
    tm = MOE_TILE if t >= 8 * MOE_TILE else 64
    tf = MOE_FF_TILE
    nf = FF_EXPERT // tf
    ridx, rgate = _route(x2, g, w_router)

    n_assign = 2 * t
    expert = ridx[:, 0:2].reshape(n_assign)
    onehot = (expert[:, None] == jnp.arange(N_EXPERTS, dtype=jnp.int32)[None, :]).astype(jnp.int32)
    csum = jnp.cumsum(onehot, axis=0)
    rank = jnp.sum((csum - onehot) * onehot, axis=1)
    padded = ((csum[-1] + tm - 1) // tm) * tm
    pend = jnp.cumsum(padded)
    dest = jnp.sum(onehot * (pend - padded)[None, :], axis=1) + rank
    n_slots = n_assign + N_EXPERTS * tm
    n_tiles = n_slots // tm
    src = jnp.zeros((n_slots,), jnp.int32).at[dest].set(jnp.arange(n_assign, dtype=jnp.int32) // 2)
    tile_start = jnp.arange(n_tiles, dtype=jnp.int32) * tm
    tile_expert = jnp.minimum(jnp.sum((tile_start[:, None] >= pend[None, :]).astype(jnp.int32), axis=1),
                              N_EXPERTS - 1)
    counts = csum[-1]
    group_end = (pend - padded + counts)[tile_expert]
    tile_valid = jnp.where(tile_start < pend[-1], jnp.clip(group_end - tile_start, 0, tm), 0).astype(jnp.int32)

    wcol = lambda i, f, tv: jnp.where(tv[i] != 0, f, nf - 1)
    ys = pl.pallas_call(
        functools.partial(_expert_kernel, tm=tm, n_tiles=n_tiles),
        grid_spec=pltpu.PrefetchScalarGridSpec(
            num_scalar_prefetch=3,
            grid=(n_tiles, nf),
            in_specs=[
                pl.BlockSpec(memory_space=pl.ANY),
                pl.BlockSpec((1, d), lambda i, f, src, te, tv: (0, 0)),
                pl.BlockSpec((None, d, tf), lambda i, f, src, te, tv: (te[i], 0, wcol(i, f, tv))),
                pl.BlockSpec((None, d, tf), lambda i, f, src, te, tv: (te[i], 0, wcol(i, f, tv) + nf)),
                pl.BlockSpec((None, tf, d), lambda i, f, src, te, tv: (te[i], wcol(i, f, tv), 0)),
            ],
            out_specs=pl.BlockSpec((tm, d), lambda i, f, src, te, tv: (i, 0)),
            scratch_shapes=[
                pltpu.VMEM((tm, d), F32),
                pltpu.VMEM((tm, d), BF16),
                pltpu.SemaphoreType.DMA((1,)),
            ],
        ),
        out_shape=jax.ShapeDtypeStruct((n_slots, d), F32),
        compiler_params=_params(("arbitrary", "arbitrary")),
        name="moe_experts",
    )(src, tile_expert, tile_valid, x2, g.reshape(1, d), wgu_bf, wgu_bf, wd_bf)

    tc = min(COMBINE_TILE, t)
    nc = t // tc
    return pl.pallas_call(
        functools.partial(_combine_kernel, tm=tc, n_tiles=nc),
        grid_spec=pltpu.PrefetchScalarGridSpec(
            num_scalar_prefetch=1,
            grid=(nc,),
            in_specs=[
                pl.BlockSpec((tc, d), lambda i, dst: (i, 0)),
                pl.BlockSpec((tc, LANES), lambda i, dst: (i, 0)),
                pl.BlockSpec((1, d), lambda i, dst: (0, 0)),
                pl.BlockSpec(memory_space=pl.ANY),
            ],
            out_specs=pl.BlockSpec((tc, d), lambda i, dst: (i, 0)),
            scratch_shapes=[
                pltpu.VMEM((2, 2, tc, d), F32),
                pltpu.SemaphoreType.DMA((2,)),
            ],
        ),
        out_shape=jax.ShapeDtypeStruct((t, d), F32),
        compiler_params=_params(("arbitrary",)),
        name="moe_combine",
    )(dest, x2, rgate, g_final.reshape(1, d), ys)


def _trunks(xs, g_mix_e, w_in_e, lambda_qk, subln_g, conv_w, w_out_e, g_ffn_e, w_gate_up, w_down,
            g_mix_o, s5_a_re, s5_a_im, s5_log_step, s5_b_re, s5_b_im, s5_c_re, s5_c_im, s5_d,
            w_glu, g_ffn_o, w_router, w_exp_gate_up, w_exp_down, g_final):
    w_in = w_in_e[0].astype(BF16)
    w_out = w_out_e[0].astype(BF16)
    w_gu = w_gate_up[0].astype(BF16)
    w_dn = w_down[0].astype(BF16)
    w_gl = w_glu[0].astype(BF16)
    w_egu = w_exp_gate_up[0].astype(BF16)
    w_edn = w_exp_down[0].astype(BF16)

    shapes = [(x.shape[0], x.shape[1]) for x in xs]
    x1s, us = [], []
    for x in xs:
        b, s, d = x.shape
        z = _in_proj(x.reshape(b * s, d), g_mix_e[0], w_in, s).reshape(b, s, IN_COLS)
        a = _attention(z, lambda_qk[0], subln_g[0])
        x1 = _out_proj(a, z, conv_w[0], x, w_out).reshape(b * s, d)
        x1 = _ffn(x1, g_ffn_e[0], w_gu, w_dn)
        x1s.append(x1)
        us.append(_norm_t(x1, g_mix_o[0]))
    ys = _s5(us, shapes, s5_a_re[0], s5_a_im[0], s5_log_step[0], s5_b_re[0], s5_b_im[0], s5_c_re[0], s5_c_im[0],
             s5_d[0])
    outs = []
    for x1, y, (b, s) in zip(x1s, ys, shapes):
        x2 = _glu(x1, y, w_gl)
        out = _moe(x2, g_ffn_o[0], w_router[0], w_egu, w_edn, g_final)
        outs.append(out.reshape(b, s, D_MODEL))
    return outs


def kernel(x_prompt, x_sample, g_mix_e, w_in_e, lambda_qk, subln_g, conv_w, w_out_e, g_ffn_e, w_gate_up, w_down, g_mix_o, s5_a_re, s5_a_im, s5_log_step, s5_b_re, s5_b_im, s5_c_re, s5_c_im, s5_d, w_glu, g_ffn_o, w_router, w_exp_gate_up, w_exp_down, g_final):
    y_prompt, y_sample = _trunks(
        [x_prompt, x_sample], g_mix_e, w_in_e, lambda_qk, subln_g, conv_w, w_out_e, g_ffn_e, w_gate_up, w_down,
        g_mix_o, s5_a_re, s5_a_im, s5_log_step, s5_b_re, s5_b_im, s5_c_re, s5_c_im, s5_d,
        w_glu, g_ffn_o, w_router, w_exp_gate_up, w_exp_down, g_final)
    return (y_prompt, y_sample)
```
